```python
import math
import jax, jax.numpy as jnp
from jax import lax
import numpy as np

D_MODEL = 1024
BATCH = 32
SEQ = 2048
DEPTH = 2

EPS = 1e-6
ROPE_THETA = 10000.0
D_FF = 2816
CONV_K = 4
BLOCK = 128

D_MIX = 2 * D_MODEL
SSD_WIDTH = D_MIX // 2
SSD_HEAD_DIM = 64
SSD_HEADS = SSD_WIDTH // SSD_HEAD_DIM
SSD_GROUPS = 2
SSD_STATE = 128
SSD_CONV_DIM = SSD_WIDTH + 2 * SSD_GROUPS * SSD_STATE
SSD_CHUNK = 128
ML_WIDTH = D_MIX - SSD_WIDTH
ML_HEADS = 4
ML_V_DIM = ML_WIDTH // ML_HEADS
ML_QK_DIM = ML_V_DIM // 2
ML_CHUNK = 128
HY_SPLITS = (SSD_WIDTH, SSD_CONV_DIM, SSD_HEADS, ML_HEADS * ML_QK_DIM, ML_HEADS * ML_QK_DIM,
             ML_WIDTH, ML_WIDTH, ML_HEADS, ML_HEADS)
HY_IN = sum(HY_SPLITS)

ATT_HEAD_DIM = 64
ATT_HEADS = D_MODEL // ATT_HEAD_DIM
IDX_HEADS = ATT_HEADS // 2
IDX_DIM = ATT_HEAD_DIM
TOPK_MAX = 256
SA_SPLITS = (ATT_HEADS * ATT_HEAD_DIM, ATT_HEAD_DIM, ATT_HEAD_DIM, IDX_HEADS * IDX_DIM, IDX_DIM, IDX_HEADS)
SA_IN = sum(SA_SPLITS)

N_EVEN = (DEPTH + 1) // 2
N_ODD = DEPTH // 2

kernel_name = "hybrid_ssd_mlstm_dsa_macaron"


def split_cols(y, sizes):
    bounds, acc = [], 0
    for s in sizes[:-1]:
        acc += s
        bounds.append(acc)
    return jnp.split(y, bounds, axis=-1)


def rms_norm(x, g):
    xf = x.astype(jnp.float32)
    y = xf * lax.rsqrt(jnp.mean(xf * xf, axis=-1, keepdims=True) + EPS) * g.astype(jnp.float32)
    return y.astype(x.dtype)


def swiglu(h, w_gate, w_up, w_down):
    return (jax.nn.silu(h @ w_gate) * (h @ w_up)) @ w_down


def rope_tables(positions, dim):
    inv = ROPE_THETA ** (-jnp.arange(0, dim, 2, dtype=jnp.float32) / dim)
    ang = positions.astype(jnp.float32)[..., None] * inv
    return jnp.cos(ang), jnp.sin(ang)


def apply_rope(x, cos, sin):
    half = x.shape[-1] // 2
    xf = x.astype(jnp.float32)
    x1, x2 = xf[..., :half], xf[..., half:]
    c, s = cos[:, :, None, :], sin[:, :, None, :]
    return jnp.concatenate([x1 * c - x2 * s, x1 * s + x2 * c], axis=-1).astype(x.dtype)


def causal_dwconv(u, w, bias):
    L, K = u.shape[1], w.shape[0]
    up = jnp.pad(u, ((0, 0), (K - 1, 0), (0, 0)))
    return sum(up[:, k:k + L] * w[k] for k in range(K)) + bias


def ssd_chunked(x, dt, A, B, C, chunk):
    b, L, H, P = x.shape
    G, N = B.shape[-2:]
    E, nc = H // G, L // chunk
    a = (dt * A).reshape(b, nc, chunk, G, E)
    xd = (x * dt[..., None]).reshape(b, nc, chunk, G, E, P)
    B = B.reshape(b, nc, chunk, G, N)
    C = C.reshape(b, nc, chunk, G, N)
    acs = jnp.cumsum(a, axis=2)
    seg = acs[:, :, :, None] - acs[:, :, None, :]
    mask = jnp.tril(jnp.ones((chunk, chunk), bool))[:, :, None, None]
    Lm = jnp.exp(jnp.where(mask, seg, -jnp.inf))
    CB = jnp.einsum('bctgn,bcsgn->bctsg', C, B)
    y_diag = jnp.einsum('bctsg,bctsge,bcsgep->bctgep', CB, Lm, xd)
    decay_end = jnp.exp(acs[:, :, -1:] - acs)
    s_loc = jnp.einsum('bcsgn,bcsge,bcsgep->bcgepn', B, decay_end, xd)
    chunk_decay = jnp.exp(acs[:, :, -1])

    def step(hc, inp):
        dec, s = inp
        return hc * dec[..., None, None] + s, hc

    h0 = jnp.zeros((b, G, E, P, N), jnp.float32)
    _, h_in = lax.scan(step, h0, (jnp.moveaxis(chunk_decay, 1, 0), jnp.moveaxis(s_loc, 1, 0)))
    h_in = jnp.moveaxis(h_in, 0, 1)
    y_off = jnp.einsum('bctgn,bcgepn,bctge->bctgep', C, h_in, jnp.exp(acs))
    return (y_diag + y_off).reshape(b, L, H, P)


def mlstm_chunked(q, k, v, i_pre, f_pre, chunk):
    b, L, H, dk = q.shape
    dv = v.shape[-1]
    nc = L // chunk
    q = (q.astype(jnp.float32) * dk ** -0.5).reshape(b, nc, chunk, H, dk)
    k = k.astype(jnp.float32).reshape(b, nc, chunk, H, dk)
    v = v.astype(jnp.float32).reshape(b, nc, chunk, H, dv)
    li = i_pre.astype(jnp.float32).reshape(b, nc, chunk, H)
    lf = jax.nn.log_sigmoid(f_pre.astype(jnp.float32)).reshape(b, nc, chunk, H)
    bcs = jnp.cumsum(lf, axis=2)
    b_last = bcs[:, :, -1]
    g = b_last[:, :, None] - bcs + li
    m_loc = jnp.max(g, axis=2)
    wg = jnp.exp(g - m_loc[:, :, None])
    c_loc = jnp.einsum('bcsh,bcshv,bcshk->bchvk', wg, v, k)
    n_loc = jnp.einsum('bcsh,bcshk->bchk', wg, k)

    def step(carry, inp):
        Cs, ns, ms = carry
        bl, cl, nl, ml = inp
        m_new = jnp.maximum(bl + ms, ml)
        a = jnp.exp(bl + ms - m_new)
        gg = jnp.exp(ml - m_new)
        new = (a[..., None, None] * Cs + gg[..., None, None] * cl,
               a[..., None] * ns + gg[..., None] * nl, m_new)
        return new, (Cs, ns, ms)

    init = (jnp.zeros((b, H, dv, dk), jnp.float32), jnp.zeros((b, H, dk), jnp.float32),
            jnp.zeros((b, H), jnp.float32))
    xs = tuple(jnp.moveaxis(t, 1, 0) for t in (b_last, c_loc, n_loc, m_loc))
    _, (c_in, n_in, m_in) = lax.scan(step, init, xs)
    c_in, n_in, m_in = (jnp.moveaxis(t, 0, 1) for t in (c_in, n_in, m_in))
    Dm = bcs[:, :, :, None, :] - bcs[:, :, None, :, :] + li[:, :, None, :, :]
    mask = jnp.tril(jnp.ones((chunk, chunk), bool))[:, :, None]
    Dm = jnp.where(mask, Dm, -jnp.inf)
    inter = bcs + m_in[:, :, None, :]
    m_t = jnp.maximum(jnp.max(Dm, axis=3), inter)
    S = jnp.einsum('bcthk,bcshk->bctsh', q, k) * jnp.exp(Dm - m_t[:, :, :, None, :])
    w_inter = jnp.exp(inter - m_t)
    num = jnp.einsum('bctsh,bcshv->bcthv', S, v) + w_inter[..., None] * jnp.einsum('bcthk,bchvk->bcthv', q, c_in)
    den = jnp.sum(S, axis=3) + w_inter * jnp.einsum('bcthk,bchk->bcth', q, n_in)
    h = num / jnp.maximum(jnp.abs(den), jnp.exp(-m_t))[..., None]
    return h.reshape(b, L, H, dv)


def hybrid_mixer(h, w_in, conv_w, conv_b, dt_bias, a_log, d_skip, ssd_norm,
                 igate_bias, fgate_bias, mlstm_norm, w_out):
    b, L, _ = h.shape
    z, xbc, dt_raw, qm, km, vm, om, ig, fg = split_cols(h @ w_in, HY_SPLITS)
    xbc = jax.nn.silu(causal_dwconv(xbc, conv_w, conv_b))
    xs, bm, cm = split_cols(xbc, (SSD_WIDTH, SSD_GROUPS * SSD_STATE, SSD_GROUPS * SSD_STATE))
    xs = xs.reshape(b, L, SSD_HEADS, SSD_HEAD_DIM).astype(jnp.float32)
    dt = jax.nn.softplus(dt_raw.astype(jnp.float32) + dt_bias.astype(jnp.float32))
    A = -jnp.exp(a_log.astype(jnp.float32))
    y = ssd_chunked(xs, dt, A,
                    bm.reshape(b, L, SSD_GROUPS, SSD_STATE).astype(jnp.float32),
                    cm.reshape(b, L, SSD_GROUPS, SSD_STATE).astype(jnp.float32), SSD_CHUNK)
    y = y + xs * d_skip.astype(jnp.float32)[:, None]
    gate = jax.nn.silu(z.astype(jnp.float32)).reshape(b, L, SSD_GROUPS, -1)
    y = rms_norm(y.reshape(b, L, SSD_GROUPS, -1) * gate, ssd_norm.reshape(SSD_GROUPS, -1))
    y = y.reshape(b, L, SSD_WIDTH)
    hm = mlstm_chunked(qm.reshape(b, L, ML_HEADS, ML_QK_DIM), km.reshape(b, L, ML_HEADS, ML_QK_DIM),
                       vm.reshape(b, L, ML_HEADS, ML_V_DIM), ig + igate_bias, fg + fgate_bias, ML_CHUNK)
    hm = rms_norm(hm, mlstm_norm.reshape(ML_HEADS, ML_V_DIM)).reshape(b, L, ML_WIDTH)
    hm = hm * jax.nn.sigmoid(om.astype(jnp.float32))
    return jnp.concatenate([y, hm], axis=-1).astype(h.dtype) @ w_out


def sparse_attention(h, cos, sin, w_in, q_norm, k_norm, w_out, topk):
    b, L, _ = h.shape
    q, k, v, qi, ki, wi = split_cols(h @ w_in, SA_SPLITS)
    q = apply_rope(rms_norm(q.reshape(b, L, ATT_HEADS, ATT_HEAD_DIM), q_norm), cos, sin)
    k = apply_rope(rms_norm(k, k_norm)[:, :, None, :], cos, sin)[:, :, 0]
    qi = apply_rope(qi.reshape(b, L, IDX_HEADS, IDX_DIM), cos, sin)
    ki = apply_rope(ki[:, :, None, :], cos, sin)[:, :, 0]
    kv = jnp.concatenate([k, v], axis=-1)
    nb = L // BLOCK
    key_pos = jnp.arange(L)
    idx_scale = (IDX_HEADS ** -0.5) * (IDX_DIM ** -0.5)
    att_scale = ATT_HEAD_DIM ** -0.5

    def to_blocks(t):
        return jnp.moveaxis(t.reshape(b, nb, BLOCK, *t.shape[2:]), 1, 0)

    def block(args):
        j, qb, qib, wib = args
        q_pos = j * BLOCK + jnp.arange(BLOCK)
        logits = jnp.einsum('bthd,bsd->btsh', qib, ki).astype(jnp.float32)
        score = jnp.einsum('btsh,bth->bts', jax.nn.relu(logits), wib.astype(jnp.float32)) * idx_scale
        causal = key_pos[None, :] <= q_pos[:, None]
        score = jnp.where(causal[None], score, -jnp.inf)
        _, sel = lax.top_k(score, topk)
        valid = sel <= q_pos[None, :, None]
        kv_sel = jax.vmap(lambda kvb, sb: kvb[sb])(kv, sel)
        k_sel, v_sel = kv_sel[..., :ATT_HEAD_DIM], kv_sel[..., ATT_HEAD_DIM:]
        s = jnp.einsum('bthd,btkd->bthk', qb, k_sel).astype(jnp.float32) * att_scale
        s = jnp.where(valid[:, :, None, :], s, -jnp.inf)
        p = jax.nn.softmax(s, axis=-1).astype(v_sel.dtype)
        return jnp.einsum('bthk,btkd->bthd', p, v_sel)

    o = lax.map(block, (jnp.arange(nb), to_blocks(q), to_blocks(qi), to_blocks(wi)))
    o = jnp.moveaxis(o, 0, 1).reshape(b, L, ATT_HEADS * ATT_HEAD_DIM)
    return o @ w_out


def setup_inputs(seed: int = 0) -> dict:
    key = jax.random.key(seed)
    ks = jax.random.split(key, 24)
    f32 = jnp.float32

    def nrm(k, shape, scale):
        return jax.random.normal(k, shape, f32) * scale

    def gain(k, shape):
        return 1.0 + 0.02 * jax.random.normal(k, shape, f32)

    x = nrm(ks[0], (BATCH, SEQ, D_MODEL), 1.0)
    positions = (jax.random.randint(ks[1], (BATCH, 1), 0, 4096, jnp.int32)
                 + jnp.arange(SEQ, dtype=jnp.int32)[None, :])
    dt0 = jnp.exp(jax.random.uniform(ks[10], (N_EVEN, SSD_HEADS), f32, math.log(1e-3), math.log(1e-1)))
    return {
        "x": x,
        "positions": positions,
        "ffn_norm": gain(ks[2], (DEPTH, 2, D_MODEL)),
        "ffn_w_gate": nrm(ks[3], (DEPTH, 2, D_MODEL, D_FF), D_MODEL ** -0.5),
        "ffn_w_up": nrm(ks[4], (DEPTH, 2, D_MODEL, D_FF), D_MODEL ** -0.5),
        "ffn_w_down": nrm(ks[5], (DEPTH, 2, D_FF, D_MODEL), D_FF ** -0.5),
        "mix_norm": gain(ks[6], (DEPTH, D_MODEL)),
        "hy_w_in": nrm(ks[7], (N_EVEN, D_MODEL, HY_IN), D_MODEL ** -0.5),
        "hy_conv_w": nrm(ks[8], (N_EVEN, CONV_K, SSD_CONV_DIM), CONV_K ** -0.5),
        "hy_conv_b": nrm(ks[9], (N_EVEN, SSD_CONV_DIM), 0.01),
        "hy_dt_bias": jnp.log(jnp.expm1(dt0)),
        "hy_a_log": jnp.log(jax.random.uniform(ks[11], (N_EVEN, SSD_HEADS), f32, 1.0, 16.0)),
        "hy_d_skip": gain(ks[12], (N_EVEN, SSD_HEADS)),
        "hy_ssd_norm": gain(ks[13], (N_EVEN, SSD_WIDTH)),
        "hy_igate_bias": nrm(ks[14], (N_EVEN, ML_HEADS), 0.1),
        "hy_fgate_bias": jnp.linspace(3.0, 6.0, ML_HEADS, dtype=f32)[None, :] + nrm(ks[15], (N_EVEN, ML_HEADS), 0.1),
        "hy_mlstm_norm": gain(ks[16], (N_EVEN, ML_WIDTH)),
        "hy_w_out": nrm(ks[17], (N_EVEN, D_MIX, D_MODEL), D_MIX ** -0.5),
        "sa_w_in": nrm(ks[18], (N_ODD, D_MODEL, SA_IN), D_MODEL ** -0.5),
        "sa_q_norm": gain(ks[19], (N_ODD, ATT_HEAD_DIM)),
        "sa_k_norm": gain(ks[20], (N_ODD, ATT_HEAD_DIM)),
        "sa_w_out": nrm(ks[21], (N_ODD, ATT_HEADS * ATT_HEAD_DIM, D_MODEL), (ATT_HEADS * ATT_HEAD_DIM) ** -0.5),
    }


def reference(x, positions, ffn_norm, ffn_w_gate, ffn_w_up, ffn_w_down, mix_norm,
              hy_w_in, hy_conv_w, hy_conv_b, hy_dt_bias, hy_a_log, hy_d_skip, hy_ssd_norm,
              hy_igate_bias, hy_fgate_bias, hy_mlstm_norm, hy_w_out,
              sa_w_in, sa_q_norm, sa_k_norm, sa_w_out):
    cos, sin = rope_tables(positions, ATT_HEAD_DIM)
    topk = min(TOPK_MAX, x.shape[1] // 4)
    for l in range(DEPTH):
        x = x + 0.5 * swiglu(rms_norm(x, ffn_norm[l, 0]), ffn_w_gate[l, 0], ffn_w_up[l, 0], ffn_w_down[l, 0])
        h = rms_norm(x, mix_norm[l])
        if l % 2 == 0:
            e = l // 2
            m = hybrid_mixer(h, hy_w_in[e], hy_conv_w[e], hy_conv_b[e], hy_dt_bias[e], hy_a_log[e],
                             hy_d_skip[e], hy_ssd_norm[e], hy_igate_bias[e], hy_fgate_bias[e],
                             hy_mlstm_norm[e], hy_w_out[e])
        else:
            o = l // 2
            m = sparse_attention(h, cos, sin, sa_w_in[o], sa_q_norm[o], sa_k_norm[o], sa_w_out[o], topk)
        x = x + m
        x = x + 0.5 * swiglu(rms_norm(x, ffn_norm[l, 1]), ffn_w_gate[l, 1], ffn_w_up[l, 1], ffn_w_down[l, 1])
    return x
```

```python
import functools

import jax
import jax.numpy as jnp
import numpy as np
from jax import lax
from jax.experimental import pallas as pl
from jax.experimental.pallas import tpu as pltpu

F32 = jnp.float32
BF16 = jnp.bfloat16
I32 = jnp.int32

EPS = 1e-6
ROPE_THETA = 10000.0
LANES = 128
SUBLANES = 8
CHUNK = 128
CONV_K = 4
TOPK_MAX = 256

D_MODEL = 1024
D_FF = 2816
SSD_WIDTH = 1024
SSD_HEAD_DIM = 64
SSD_HEADS = 16
SSD_GROUPS = 2
SSD_STATE = 128
SSD_CONV_DIM = SSD_WIDTH + 2 * SSD_GROUPS * SSD_STATE
ML_WIDTH = 1024
ML_HEADS = 4
ML_V_DIM = 256
ML_QK_DIM = 128
HY_SPLITS = (SSD_WIDTH, SSD_CONV_DIM, SSD_HEADS, ML_HEADS * ML_QK_DIM, ML_HEADS * ML_QK_DIM,
             ML_WIDTH, ML_WIDTH, ML_HEADS, ML_HEADS)
ATT_HEAD_DIM = 64
ATT_HEADS = 16
IDX_HEADS = 8
IDX_DIM = 64
SA_SPLITS = (ATT_HEADS * ATT_HEAD_DIM, ATT_HEAD_DIM, ATT_HEAD_DIM, IDX_HEADS * IDX_DIM, IDX_DIM, IDX_HEADS)

VMEM_LIMIT_BYTES = 56 * 1024 * 1024

NT_DIMS = (((1,), (1,)), ((), ()))


def _params(*sem):
    return pltpu.CompilerParams(dimension_semantics=sem, vmem_limit_bytes=VMEM_LIMIT_BYTES)


def _resident(shape):
    nd = len(shape)
    return pl.BlockSpec(shape, lambda *_: (0,) * nd, pipeline_mode=pl.Buffered(1))


def _dot(a, b):
    return jnp.dot(a, b, preferred_element_type=F32)


def _dot_nt(a, b):
    return lax.dot_general(a, b, NT_DIMS, preferred_element_type=F32)


def _split3(a):
    a1 = a.astype(BF16)
    r = a - a1.astype(F32)
    a2 = r.astype(BF16)
    r = r - a2.astype(F32)
    return a1, a2, r.astype(BF16)


def _dot_f32_lhs(a, b01):
    p1, p2, p3 = _split3(a)
    return _dot(p1, b01) + _dot(p2, b01) + _dot(p3, b01)


def _dot_f32_rhs(a01, b):
    p1, p2, p3 = _split3(b)
    return _dot(a01, p1) + _dot(a01, p2) + _dot(a01, p3)


def _rms(x, g):
    return x * lax.rsqrt(jnp.mean(x * x, axis=-1, keepdims=True) + EPS) * g


def _sigmoid(x):
    return 1.0 / (1.0 + jnp.exp(-x))


def _softplus(x):
    return jnp.maximum(x, 0.0) + jnp.log1p(jnp.exp(-jnp.abs(x)))


def _iota(shape, dim):
    return lax.broadcasted_iota(I32, shape, dim)


def _expander(rows, cols, width):
    return (_iota((rows, cols), 1) // width == _iota((rows, cols), 0)).astype(BF16)


def _tril(n):
    return (_iota((n, n), 0) >= _iota((n, n), 1))


def _swiglu_half(x, g_ref, wg_ref, wu_ref, wd_ref, ff_chunk):
    h = _rms(x, g_ref[...]).astype(BF16)
    acc = jnp.zeros(x.shape, F32)
    for c0 in range(0, wg_ref.shape[1], ff_chunk):
        gate = _dot(h, wg_ref[:, c0:c0 + ff_chunk])
        up = _dot(h, wu_ref[:, c0:c0 + ff_chunk])
        act = (gate * _sigmoid(gate) * up).astype(BF16)
        acc = acc + _dot(act, wd_ref[c0:c0 + ff_chunk, :])
    return x + 0.5 * acc


def _ffn_body(x_ref, g_ref, wg_ref, wu_ref, wd_ref, o_ref, *, ff_chunk):
    o_ref[...] = _swiglu_half(x_ref[...], g_ref, wg_ref, wu_ref, wd_ref, ff_chunk)


def _ffn(x2d, g, wg, wu, wd, *, tm, ff_chunk=256):
    t, d = x2d.shape
    return pl.pallas_call(
        functools.partial(_ffn_body, ff_chunk=ff_chunk),
        out_shape=jax.ShapeDtypeStruct((t, d), F32),
        grid=(t // tm,),
        in_specs=[pl.BlockSpec((tm, d), lambda i: (i, 0)), _resident(g.shape),
                  _resident(wg.shape), _resident(wu.shape), _resident(wd.shape)],
        out_specs=pl.BlockSpec((tm, d), lambda i: (i, 0)),
        compiler_params=_params("parallel"),
        name="ffn",
    )(x2d, g, wg, wu, wd)


def _proj_ffn_body(*refs, n_in, ff_chunk):
    x_ref = refs[0]
    a_refs = refs[1:1 + n_in]
    w_refs = refs[1 + n_in:1 + 2 * n_in]
    g_ref, wg_ref, wu_ref, wd_ref, o_ref = refs[1 + 2 * n_in:]
    x = x_ref[...]
    for a_ref, w_ref in zip(a_refs, w_refs):
        x = x + _dot(a_ref[...], w_ref[...])
    o_ref[...] = _swiglu_half(x, g_ref, wg_ref, wu_ref, wd_ref, ff_chunk)


def _proj_ffn(x2d, acts, ws, g, wg, wu, wd, *, tm, ff_chunk=256):
    t, d = x2d.shape
    n_in = len(acts)
    return pl.pallas_call(
        functools.partial(_proj_ffn_body, n_in=n_in, ff_chunk=ff_chunk),
        out_shape=jax.ShapeDtypeStruct((t, d), F32),
        grid=(t // tm,),
        in_specs=([pl.BlockSpec((tm, d), lambda i: (i, 0))]
                  + [pl.BlockSpec((tm, a.shape[1]), lambda i: (i, 0)) for a in acts]
                  + [_resident(w.shape) for w in ws]
                  + [_resident(g.shape), _resident(wg.shape), _resident(wu.shape), _resident(wd.shape)]),
        out_specs=pl.BlockSpec((tm, d), lambda i: (i, 0)),
        compiler_params=_params("parallel"),
        name="proj_ffn",
    )(x2d, *acts, *ws, g, wg, wu, wd)


def _norm_proj_body(x_ref, g_ref, w_ref, *o_refs, widths):
    h = _rms(x_ref[...], g_ref[...]).astype(BF16)
    c0 = 0
    for o_ref, wd in zip(o_refs, widths):
        o_ref[...] = _dot(h, w_ref[:, c0:c0 + wd]).astype(o_ref.dtype)
        c0 += wd


def _norm_proj(x2d, g, w, widths, dtypes, *, tm):
    t, d = x2d.shape
    return pl.pallas_call(
        functools.partial(_norm_proj_body, widths=widths),
        out_shape=[jax.ShapeDtypeStruct((t, wd), dt) for wd, dt in zip(widths, dtypes)],
        grid=(t // tm,),
        in_specs=[pl.BlockSpec((tm, d), lambda i: (i, 0)), _resident(g.shape), _resident(w.shape)],
        out_specs=[pl.BlockSpec((tm, wd), lambda i: (i, 0)) for wd in widths],
        compiler_params=_params("parallel"),
        name="norm_proj",
    )(x2d, g, w)


def _ssd_body(z_ref, xbc_ref, dt_ref, cw_ref, cb_ref, dtb_ref, alog_ref, dskip_ref, nrm_ref, y_ref,
              ubuf, state):
    q_len = CHUNK
    hp = SSD_WIDTH // SSD_GROUPS

    @pl.when(pl.program_id(1) == 0)
    def _():
        ubuf[0:SUBLANES, :] = jnp.zeros((SUBLANES, SSD_CONV_DIM), F32)
        state[...] = jnp.zeros(state.shape, F32)

    u = xbc_ref[0]
    ubuf[SUBLANES:SUBLANES + q_len, :] = u
    conv = cb_ref[...]
    for k in range(CONV_K):
        off = SUBLANES - (CONV_K - 1) + k
        conv = conv + cw_ref[k:k + 1, :] * ubuf[off:off + q_len, :]
    ubuf[0:SUBLANES, :] = u[q_len - SUBLANES:, :]
    xc = conv * _sigmoid(conv)
    xs = xc[:, :SSD_WIDTH]
    bm = xc[:, SSD_WIDTH:SSD_WIDTH + SSD_GROUPS * SSD_STATE]
    cm = xc[:, SSD_WIDTH + SSD_GROUPS * SSD_STATE:]

    lane = _iota((q_len, LANES), 1)
    dt = jnp.where(lane < SSD_HEADS, _softplus(dt_ref[0] + dtb_ref[...]), 0.0)
    a = dt * (-jnp.exp(alog_ref[...]))
    tril = _tril(q_len)
    acs = _dot_f32_rhs(tril.astype(BF16), a)
    eacs = jnp.exp(acs)
    dend = jnp.exp(acs[q_len - 1:q_len, :] - acs)
    expand = _expander(LANES, SSD_WIDTH, SSD_HEAD_DIM)
    wide = _dot_f32_lhs(jnp.concatenate([dt, eacs, dend], axis=0), expand)
    dt_x, eacs_x, dend_x = wide[:q_len], wide[q_len:2 * q_len], wide[2 * q_len:]

    xd = xs * dt_x
    lane_w = _iota((q_len, SSD_WIDTH), 1)
    low_half = (lane_w % LANES) < SSD_HEAD_DIM
    xd_halves = (jnp.where(low_half, xd, 0.0).astype(BF16), jnp.where(low_half, 0.0, xd).astype(BF16))
    xdd_b = (xd * dend_x).astype(BF16)
    acs_t = acs.T
    st = state[...]
    st_b = st.astype(BF16)

    y_diag, y_off, s_new = [], [], []
    heads_per_group = SSD_HEADS // SSD_GROUPS
    for g in range(SSD_GROUPS):
        bg = bm[:, g * SSD_STATE:(g + 1) * SSD_STATE]
        cg_b = cm[:, g * SSD_STATE:(g + 1) * SSD_STATE].astype(BF16)
        cb = _dot_nt(cg_b, bg.astype(BF16))
        s_new.append(_dot(bg.T.astype(BF16), xdd_b[:, g * hp:(g + 1) * hp]))
        y_off.append(_dot(cg_b, st_b[:, g * hp:(g + 1) * hp]))
        for pair in range(heads_per_group // 2):
            acc = None
            for par in range(2):
                h = g * heads_per_group + 2 * pair + par
                seg = acs[:, h:h + 1] - acs_t[h:h + 1, :]
                m = (cb * jnp.exp(jnp.where(tril, seg, -jnp.inf))).astype(BF16)
                slab = (h // 2) * LANES
                d = _dot(m, xd_halves[par][:, slab:slab + LANES])
                acc = d if acc is None else acc + d
            y_diag.append(acc)
    y = (jnp.concatenate(y_diag, axis=1) + eacs_x * jnp.concatenate(y_off, axis=1) + xs * dskip_ref[...])
    state[...] = st * eacs_x[q_len - 1:q_len, :] + jnp.concatenate(s_new, axis=1)

    z = z_ref[0]
    yg = y * (z * _sigmoid(z))
    nrm = nrm_ref[...]
    y_ref[0] = jnp.concatenate(
        [_rms(yg[:, g * hp:(g + 1) * hp], nrm[:, g * hp:(g + 1) * hp]) for g in range(SSD_GROUPS)],
        axis=1).astype(y_ref.dtype)


def _ssd(z, xbc, dts, conv_w, conv_b, dt_bias, a_log, d_skip_x, ssd_norm):
    b, l, _ = z.shape
    nc = l // CHUNK
    blk = lambda w: pl.BlockSpec((1, CHUNK, w), lambda i, c: (i, c, 0))
    return pl.pallas_call(
        _ssd_body,
        out_shape=jax.ShapeDtypeStruct((b, l, SSD_WIDTH), BF16),
        grid=(b, nc),
        in_specs=[blk(SSD_WIDTH), blk(SSD_CONV_DIM), blk(LANES),
                  _resident(conv_w.shape), _resident(conv_b.shape), _resident(dt_bias.shape),
                  _resident(a_log.shape), _resident(d_skip_x.shape), _resident(ssd_norm.shape)],
        out_specs=blk(SSD_WIDTH),
        scratch_shapes=[pltpu.VMEM((SUBLANES + CHUNK, SSD_CONV_DIM), F32),
                        pltpu.VMEM((SSD_STATE, SSD_WIDTH), F32)],
        compiler_params=_params("parallel", "arbitrary"),
        name="ssd",
    )(z, xbc, dts, conv_w, conv_b, dt_bias, a_log, d_skip_x, ssd_norm)


def _mlstm_body(q_ref, k_ref, v_ref, o_ref, i_ref, f_ref, ib_ref, fb_ref, nrm_ref, h_ref, c_st, n_st, m_st):
    q_len = CHUNK

    @pl.when(pl.program_id(1) == 0)
    def _():
        c_st[...] = jnp.zeros(c_st.shape, F32)
        n_st[...] = jnp.zeros(n_st.shape, F32)
        m_st[...] = jnp.zeros(m_st.shape, F32)

    lane = _iota((q_len, LANES), 1)
    is_head = lane < ML_HEADS
    li = jnp.where(is_head, i_ref[0] + ib_ref[...], 0.0)
    lf = jnp.where(is_head, -_softplus(-(f_ref[0] + fb_ref[...])), 0.0)
    tril = _tril(q_len)
    bcs = _dot_f32_rhs(tril.astype(BF16), lf)
    b_last = bcs[q_len - 1:q_len, :]
    gl = b_last - bcs + li
    m_loc = jnp.max(gl, axis=0, keepdims=True)
    wg = jnp.exp(gl - m_loc)
    m_in = m_st[0:1, :]
    m_new = jnp.maximum(b_last + m_in, m_loc)
    a_sc = jnp.exp(b_last + m_in - m_new)
    g_sc = jnp.exp(m_loc - m_new)

    exp_v = _expander(LANES, ML_WIDTH, ML_V_DIM)
    exp_k = _expander(LANES, ML_HEADS * ML_QK_DIM, ML_QK_DIM)
    sc = jnp.concatenate([wg, jnp.broadcast_to(a_sc, (SUBLANES, LANES)),
                          jnp.broadcast_to(g_sc, (SUBLANES, LANES))], axis=0)
    sc_v = _dot_f32_lhs(sc, exp_v)
    sc_k = _dot_f32_lhs(sc, exp_k)
    wg_v, a_v, g_v = sc_v[:q_len], sc_v[q_len:q_len + 1], sc_v[q_len + SUBLANES:q_len + SUBLANES + 1]
    wg_k, a_k, g_k = sc_k[:q_len], sc_k[q_len:q_len + 1], sc_k[q_len + SUBLANES:q_len + SUBLANES + 1]

    q = q_ref[0] * (ML_QK_DIM ** -0.5)
    k = k_ref[0]
    v = v_ref[0]
    q_b, k_b, v_b = q.astype(BF16), k.astype(BF16), v.astype(BF16)
    vw_b = (v * wg_v).astype(BF16)
    n_in = n_st[0:1, :]
    n_loc = jnp.sum(k * wg_k, axis=0, keepdims=True)
    d_t = (li - bcs).T

    outs, c_new = [], []
    for h in range(ML_HEADS):
        ks = slice(h * ML_QK_DIM, (h + 1) * ML_QK_DIM)
        vs = slice(h * ML_V_DIM, (h + 1) * ML_V_DIM)
        c_new.append(_dot(k[:, ks].T.astype(BF16), vw_b[:, vs]))
        dm = jnp.where(tril, bcs[:, h:h + 1] + d_t[h:h + 1, :], -jnp.inf)
        inter = bcs[:, h:h + 1] + m_in[:, h:h + 1]
        m_t = jnp.maximum(jnp.max(dm, axis=1, keepdims=True), inter)
        s = _dot_nt(q_b[:, ks], k_b[:, ks]) * jnp.exp(dm - m_t)
        w_int = jnp.exp(inter - m_t)
        num = _dot(s.astype(BF16), v_b[:, vs]) + w_int * _dot(q_b[:, ks], c_st[:, vs].astype(BF16))
        den = (jnp.sum(s, axis=1, keepdims=True)
               + w_int * jnp.sum(q[:, ks] * n_in[:, ks], axis=1, keepdims=True))
        outs.append(num / jnp.maximum(jnp.abs(den), jnp.exp(-m_t)))
    nrm = nrm_ref[...]
    hn = jnp.concatenate([_rms(outs[h], nrm[:, h * ML_V_DIM:(h + 1) * ML_V_DIM]) for h in range(ML_HEADS)],
                         axis=1)
    h_ref[0] = (hn * _sigmoid(o_ref[0])).astype(h_ref.dtype)

    c_st[...] = a_v * c_st[...] + g_v * jnp.concatenate(c_new, axis=1)
    n_st[...] = jnp.broadcast_to(a_k * n_in + g_k * n_loc, n_st.shape)
    m_st[...] = jnp.broadcast_to(m_new, m_st.shape)


def _mlstm(q, k, v, o, ig, fg, ib, fb, nrm):
    b, l, _ = q.shape
    nc = l // CHUNK
    blk = lambda w: pl.BlockSpec((1, CHUNK, w), lambda i, c: (i, c, 0))
    kw = ML_HEADS * ML_QK_DIM
    return pl.pallas_call(
        _mlstm_body,
        out_shape=jax.ShapeDtypeStruct((b, l, ML_WIDTH), BF16),
        grid=(b, nc),
        in_specs=[blk(kw), blk(kw), blk(ML_WIDTH), blk(ML_WIDTH), blk(LANES), blk(LANES),
                  _resident(ib.shape), _resident(fb.shape), _resident(nrm.shape)],
        out_specs=blk(ML_WIDTH),
        scratch_shapes=[pltpu.VMEM((ML_QK_DIM, ML_WIDTH), F32),
                        pltpu.VMEM((SUBLANES, kw), F32),
                        pltpu.VMEM((SUBLANES, LANES), F32)],
        compiler_params=_params("parallel", "arbitrary"),
        name="mlstm",
    )(q, k, v, o, ig, fg, ib, fb, nrm)


def _rope(x, cos, sin, first_half):
    partner = jnp.where(first_half, pltpu.roll(x, LANES - ATT_HEAD_DIM // 2, 1), pltpu.roll(x, ATT_HEAD_DIM // 2, 1))
    return x * cos + partner * sin


def _sa_proj_body(x_ref, g_ref, w_ref, gq_ref, gk_ref, cos_ref, sin_ref,
                  q_ref, qi_ref, ka_ref, kb_ref, va_ref, vb_ref, wt_ref):
    tm = x_ref.shape[0]
    nq = ATT_HEADS * ATT_HEAD_DIM
    nqi = IDX_HEADS * IDX_DIM
    h = _rms(x_ref[...], g_ref[...]).astype(BF16)
    proj = _dot(h, w_ref[...])
    cos, sin = cos_ref[...], sin_ref[...]
    lane = _iota((tm, LANES), 1)
    first_half = (lane % ATT_HEAD_DIM) < ATT_HEAD_DIM // 2
    low_head = lane < ATT_HEAD_DIM

    q = proj[:, :nq]
    seg_sum = (_iota((nq, LANES), 0) // ATT_HEAD_DIM == _iota((nq, LANES), 1)).astype(BF16)
    ms = _dot_f32_lhs(q * q, seg_sum) * (1.0 / ATT_HEAD_DIM)
    rs = _dot_f32_lhs(lax.rsqrt(ms + EPS), _expander(LANES, nq, ATT_HEAD_DIM))
    qn = q * rs * gq_ref[...]
    att_scale = ATT_HEAD_DIM ** -0.5
    for s0 in range(0, nq, LANES):
        q_ref[:, s0:s0 + LANES] = (_rope(qn[:, s0:s0 + LANES], cos, sin, first_half) * att_scale).astype(BF16)
    for s0 in range(0, nqi, LANES):
        qi_ref[:, s0:s0 + LANES] = _rope(proj[:, nq + s0:nq + s0 + LANES], cos, sin, first_half).astype(BF16)

    kk = proj[:, nq + nqi:nq + nqi + LANES]
    k_ms = jnp.sum(jnp.where(low_head, kk * kk, 0.0), axis=1, keepdims=True) * (1.0 / ATT_HEAD_DIM)
    kk = jnp.where(low_head, kk * lax.rsqrt(k_ms + EPS) * gk_ref[...], kk)
    kk = _rope(kk, cos, sin, first_half)
    ka_ref[...] = kk.astype(BF16)
    kb_ref[...] = pltpu.roll(kk, ATT_HEAD_DIM, 1).astype(BF16)
    vv = proj[:, nq + nqi + LANES:nq + nqi + 2 * LANES]
    va_ref[...] = vv.astype(BF16)
    vb_ref[...] = pltpu.roll(vv, ATT_HEAD_DIM, 1).astype(BF16)
    wt_ref[...] = proj[:, nq + nqi + 2 * LANES:].T[:SUBLANES, :]


def _sa_proj(x2d, g, w, gq, gk, cos, sin, *, tm):
    t, d = x2d.shape
    nq = ATT_HEADS * ATT_HEAD_DIM
    nqi = IDX_HEADS * IDX_DIM
    row = lambda wd: pl.BlockSpec((tm, wd), lambda i: (i, 0))
    outs = [jax.ShapeDtypeStruct((t, nq), BF16), jax.ShapeDtypeStruct((t, nqi), BF16)]
    outs += [jax.ShapeDtypeStruct((t, LANES), BF16)] * 4
    outs += [jax.ShapeDtypeStruct((SUBLANES, t), F32)]
    return pl.pallas_call(
        _sa_proj_body,
        out_shape=outs,
        grid=(t // tm,),
        in_specs=[row(d), _resident(g.shape), _resident(w.shape), _resident(gq.shape), _resident(gk.shape),
                  row(LANES), row(LANES)],
        out_specs=[row(nq), row(nqi), row(LANES), row(LANES), row(LANES), row(LANES),
                   pl.BlockSpec((SUBLANES, tm), lambda i: (0, i))],
        compiler_params=_params("parallel"),
        name="sa_proj",
    )(x2d, g, w, gq, gk, cos, sin)


INT_MIN = -2 ** 31


def _sortable_key(x):
    bits = pltpu.bitcast(jnp.where(x == 0.0, 0.0, x), I32)
    return bits ^ ((bits >> 31) & 0x7FFFFFFF)


def _dsa_body(q_ref, qi_ref, wt_ref, ka_ref, kb_ref, va_ref, vb_ref, o_ref,
              qe, qo, qie, qio, skey, bias, m_e, l_e, acc_e, m_o, l_o, acc_o, *, topk, seq_bits):
    blk = CHUNK
    j = pl.program_id(1)
    n_tiles = j + 1
    lane = _iota((blk, LANES), 1)
    row = _iota((blk, LANES), 0)
    low = lane < ATT_HEAD_DIM
    q_pos = j * blk + lane

    n_pairs = ATT_HEADS // 2
    for p in range(n_pairs):
        slab = q_ref[0, :, p * LANES:(p + 1) * LANES].astype(F32)
        qe[p * blk:(p + 1) * blk, :] = jnp.where(low, slab, 0.0).astype(BF16)
        qo[p * blk:(p + 1) * blk, :] = jnp.where(low, 0.0, slab).astype(BF16)
    n_ipairs = IDX_HEADS // 2
    for p in range(n_ipairs):
        slab = qi_ref[0, :, p * LANES:(p + 1) * LANES].astype(F32)
        qie[p * blk:(p + 1) * blk, :] = jnp.where(low, slab, 0.0).astype(BF16)
        qio[p * blk:(p + 1) * blk, :] = jnp.where(low, 0.0, slab).astype(BF16)

    idx_scale = (IDX_HEADS ** -0.5) * (IDX_DIM ** -0.5)
    wt = wt_ref[0]

    def score_tile(kt, _):
        ks = pl.ds(pl.multiple_of(kt * blk, blk), blk)
        le = _dot_nt(kb_ref[0, ks, :], qie[...])
        lo = _dot_nt(ka_ref[0, ks, :], qio[...])
        sc = jnp.zeros((blk, LANES), F32)
        for p in range(n_ipairs):
            sc = sc + jnp.maximum(le[:, p * blk:(p + 1) * blk], 0.0) * wt[2 * p:2 * p + 1, :]
            sc = sc + jnp.maximum(lo[:, p * blk:(p + 1) * blk], 0.0) * wt[2 * p + 1:2 * p + 2, :]
        sc = sc * idx_scale
        sc = jnp.where(kt * blk + row <= q_pos, sc, -jnp.inf)
        skey[ks, :] = _sortable_key(sc)
        return 0

    lax.fori_loop(0, n_tiles, score_tile, 0)

    def count(pred_fn):
        def body(kt, c):
            ks = pl.ds(pl.multiple_of(kt * blk, blk), blk)
            return c + jnp.sum(jnp.where(pred_fn(skey[ks, :], kt), 1.0, 0.0), axis=0, keepdims=True)
        return lax.fori_loop(0, n_tiles, body, jnp.zeros((1, LANES), F32))

    kf = float(topk)
    thr = jnp.full((1, LANES), INT_MIN, I32)
    for bit in range(31, -1, -1):
        cand = thr + jnp.int32(INT_MIN if bit == 31 else 1 << bit)
        c = count(lambda key, kt, cand=cand: key >= cand)
        thr = jnp.where(c >= kf, cand, thr)

    need = kf - count(lambda key, kt: key > thr)
    last = jnp.zeros((1, LANES), I32)
    for bit in range(seq_bits - 1, -1, -1):
        cand = last + jnp.int32(1 << bit)
        c = count(lambda key, kt, cand=cand: (key == thr) & (kt * blk + row < cand))
        last = jnp.where(c <= need - 1.0, cand, last)

    def bias_tile(kt, _):
        ks = pl.ds(pl.multiple_of(kt * blk, blk), blk)
        key = skey[ks, :]
        pos = kt * blk + row
        sel = ((key > thr) | ((key == thr) & (pos <= last))) & (pos <= q_pos)
        bias[kt] = jnp.where(sel, 0.0, -jnp.inf).T
        return 0

    lax.fori_loop(0, n_tiles, bias_tile, 0)

    rows = n_pairs * blk
    for m_ref, l_ref, acc_ref in ((m_e, l_e, acc_e), (m_o, l_o, acc_o)):
        m_ref[...] = jnp.full(m_ref.shape, -1e30, F32)
        l_ref[...] = jnp.zeros(l_ref.shape, F32)
        acc_ref[...] = jnp.zeros(acc_ref.shape, F32)

    def attn_tile(kt, _):
        ks = pl.ds(pl.multiple_of(kt * blk, blk), blk)
        b = bias[kt]
        for qs, k_ref, v_ref, m_ref, l_ref, acc_ref in ((qe, ka_ref, va_ref, m_e, l_e, acc_e),
                                                        (qo, kb_ref, vb_ref, m_o, l_o, acc_o)):
            s = _dot_nt(qs[...], k_ref[0, ks, :])
            s = (s.reshape(n_pairs, blk, blk) + b[None]).reshape(rows, blk)
            m_prev = m_ref[...]
            m_cur = jnp.maximum(m_prev, jnp.max(s, axis=1, keepdims=True))
            alpha = jnp.exp(m_prev - m_cur)
            p = jnp.exp(s - m_cur)
            l_ref[...] = alpha * l_ref[...] + jnp.sum(p, axis=1, keepdims=True)
            acc_ref[...] = alpha * acc_ref[...] + _dot(p.astype(BF16), v_ref[0, ks, :])
            m_ref[...] = m_cur
        return 0

    lax.fori_loop(0, n_tiles, attn_tile, 0)

    out_e = acc_e[...] / l_e[...]
    out_o = acc_o[...] / l_o[...]
    for p in range(n_pairs):
        o_ref[0, :, p * LANES:(p + 1) * LANES] = (out_e[p * blk:(p + 1) * blk, :]
                                                  + out_o[p * blk:(p + 1) * blk, :]).astype(o_ref.dtype)


def _dsa(q, qi, wt, ka, kb, va, vb, *, topk):
    b, l, nq = q.shape
    nqi = qi.shape[2]
    nb = l // CHUNK
    rows = (ATT_HEADS // 2) * CHUNK
    irows = (IDX_HEADS // 2) * CHUNK
    qblk = lambda w: pl.BlockSpec((1, CHUNK, w), lambda i, j: (i, j, 0))
    seq = pl.BlockSpec((1, l, LANES), lambda i, j: (i, 0, 0))
    return pl.pallas_call(
        functools.partial(_dsa_body, topk=topk, seq_bits=int(np.log2(l))),
        out_shape=jax.ShapeDtypeStruct((b, l, nq), BF16),
        grid=(b, nb),
        in_specs=[qblk(nq), qblk(nqi), pl.BlockSpec((1, SUBLANES, CHUNK), lambda i, j: (i, 0, j)),
                  seq, seq, seq, seq],
        out_specs=qblk(nq),
        scratch_shapes=[pltpu.VMEM((rows, LANES), BF16), pltpu.VMEM((rows, LANES), BF16),
                        pltpu.VMEM((irows, LANES), BF16), pltpu.VMEM((irows, LANES), BF16),
                        pltpu.VMEM((l, LANES), I32), pltpu.VMEM((nb, CHUNK, CHUNK), F32),
                        pltpu.VMEM((rows, 1), F32), pltpu.VMEM((rows, 1), F32), pltpu.VMEM((rows, LANES), F32),
                        pltpu.VMEM((rows, 1), F32), pltpu.VMEM((rows, 1), F32), pltpu.VMEM((rows, LANES), F32)],
        compiler_params=_params("parallel", "arbitrary"),
        name="dsa",
    )(q, qi, wt, ka, kb, va, vb)


def _split_cols(w, sizes):
    out, c0 = [], 0
    for s in sizes:
        out.append(w[:, c0:c0 + s])
        c0 += s
    return out


def _pad_cols(w, width):
    return jnp.pad(w, ((0, 0), (0, width - w.shape[1])))


def _pad_row(v, width):
    return jnp.pad(v, (0, width - v.shape[0]))[None, :]


def _hybrid_mixer_parts(x2d, b, l, mix_norm, w_in, conv_w, conv_b, dt_bias, a_log, d_skip, ssd_norm,
                        igate_bias, fgate_bias, mlstm_norm, *, tm):
    wz, wxbc, wdt, wq, wk, wv, wo, wi, wf = _split_cols(w_in, HY_SPLITS)
    w = jnp.concatenate([wz, wxbc, wq, wk, wv, wo, _pad_cols(wdt, LANES), _pad_cols(wi, LANES),
                         _pad_cols(wf, LANES)], axis=1).astype(BF16)
    kw = ML_HEADS * ML_QK_DIM
    widths = (SSD_WIDTH, SSD_CONV_DIM, kw, kw, ML_WIDTH, ML_WIDTH, LANES, LANES, LANES)
    z, xbc, qm, km, vm, om, dts, ig, fg = _norm_proj(x2d, mix_norm[None, :], w, widths, (F32,) * len(widths), tm=tm)
    r3 = lambda a: a.reshape(b, l, a.shape[1])
    y = _ssd(r3(z), r3(xbc), r3(dts), conv_w, conv_b[None, :], _pad_row(dt_bias, LANES), _pad_row(a_log, LANES),
             jnp.repeat(d_skip, SSD_HEAD_DIM)[None, :], ssd_norm[None, :])
    hm = _mlstm(r3(qm), r3(km), r3(vm), r3(om), r3(ig), r3(fg), _pad_row(igate_bias, LANES),
                _pad_row(fgate_bias, LANES), mlstm_norm[None, :])
    return y.reshape(b * l, SSD_WIDTH), hm.reshape(b * l, ML_WIDTH)


def _rope_tables(positions):
    half = ATT_HEAD_DIM // 2
    inv = ROPE_THETA ** (-jnp.arange(0, ATT_HEAD_DIM, 2, dtype=F32) / ATT_HEAD_DIM)
    ang = positions.astype(F32)[..., None] * inv
    cos, sin = jnp.cos(ang), jnp.sin(ang)
    reps = LANES // half
    cos_t = jnp.tile(cos, (1, 1, reps))
    sin_t = jnp.tile(jnp.concatenate([-sin, sin], axis=-1), (1, 1, reps // 2))
    t = positions.shape[0] * positions.shape[1]
    return cos_t.reshape(t, LANES), sin_t.reshape(t, LANES)


def _sparse_attention_parts(x2d, b, l, cos, sin, mix_norm, w_in, q_norm, k_norm, topk, *, tm):
    wq, wk, wv, wqi, wki, wwi = _split_cols(w_in, SA_SPLITS)
    w = jnp.concatenate([wq, wqi, wk, wki, _pad_cols(wv, LANES), _pad_cols(wwi, LANES)], axis=1).astype(BF16)
    gq = jnp.tile(q_norm, ATT_HEADS)[None, :]
    gk = _pad_row(k_norm, LANES)
    q, qi, ka, kb, va, vb, wt = _sa_proj(x2d, mix_norm[None, :], w, gq, gk, cos, sin, tm=tm)
    r3 = lambda a: a.reshape(b, l, a.shape[1])
    wt3 = wt.reshape(SUBLANES, b, l).transpose(1, 0, 2)
    o = _dsa(r3(q), r3(qi), wt3, r3(ka), r3(kb), r3(va), r3(vb), topk=topk)
    return o.reshape(b * l, ATT_HEADS * ATT_HEAD_DIM)


def kernel(x, positions, ffn_norm, ffn_w_gate, ffn_w_up, ffn_w_down, mix_norm, hy_w_in, hy_conv_w, hy_conv_b,
           hy_dt_bias, hy_a_log, hy_d_skip, hy_ssd_norm, hy_igate_bias, hy_fgate_bias, hy_mlstm_norm, hy_w_out,
           sa_w_in, sa_q_norm, sa_k_norm, sa_w_out):
    b, l, d = x.shape
    depth = ffn_norm.shape[0]
    t = b * l
    tm = 512 if t % 512 == 0 else CHUNK
    topk = min(TOPK_MAX, l // 4)
    cos, sin = _rope_tables(positions)
    wg, wu, wd = ffn_w_gate.astype(BF16), ffn_w_up.astype(BF16), ffn_w_down.astype(BF16)
    x2d = x.reshape(t, d)
    for layer in range(depth):
        x2d = _ffn(x2d, ffn_norm[layer, 0][None, :], wg[layer, 0], wu[layer, 0], wd[layer, 0], tm=tm)
        if layer % 2 == 0:
            e = layer // 2
            y, hm = _hybrid_mixer_parts(x2d, b, l, mix_norm[layer], hy_w_in[e], hy_conv_w[e], hy_conv_b[e],
                                        hy_dt_bias[e], hy_a_log[e], hy_d_skip[e], hy_ssd_norm[e],
                                        hy_igate_bias[e], hy_fgate_bias[e], hy_mlstm_norm[e], tm=tm)
            w_out = hy_w_out[e].astype(BF16)
            acts, ws = [y, hm], [w_out[:SSD_WIDTH], w_out[SSD_WIDTH:]]
        else:
            o = layer // 2
            att = _sparse_attention_parts(x2d, b, l, cos, sin, mix_norm[layer], sa_w_in[o], sa_q_norm[o],
                                          sa_k_norm[o], topk, tm=tm)
            acts, ws = [att], [sa_w_out[o].astype(BF16)]
        x2d = _proj_ffn(x2d, acts, ws, ffn_norm[layer, 1][None, :], wg[layer, 1], wu[layer, 1], wd[layer, 1], tm=tm)
    return x2d.reshape(b, l, d)
```

```python
import functools

import jax
import jax.numpy as jnp
import numpy as np
from jax import lax
from jax.experimental import pallas as pl
from jax.experimental.pallas import tpu as pltpu

F32 = jnp.float32
BF16 = jnp.bfloat16
I32 = jnp.int32

EPS = 1e-6
ROPE_THETA = 10000.0
LANES = 128
SUBLANES = 8
CHUNK = 128
CONV_K = 4
TOPK_MAX = 256

D_MODEL = 1024
D_FF = 2816
SSD_WIDTH = 1024
SSD_HEAD_DIM = 64
SSD_HEADS = 16
SSD_GROUPS = 2
SSD_STATE = 128
SSD_CONV_DIM = SSD_WIDTH + 2 * SSD_GROUPS * SSD_STATE
ML_WIDTH = 1024
ML_HEADS = 4
ML_V_DIM = 256
ML_QK_DIM = 128
HY_SPLITS = (SSD_WIDTH, SSD_CONV_DIM, SSD_HEADS, ML_HEADS * ML_QK_DIM, ML_HEADS * ML_QK_DIM,
             ML_WIDTH, ML_WIDTH, ML_HEADS, ML_HEADS)
ATT_HEAD_DIM = 64
ATT_HEADS = 16
IDX_HEADS = 8
IDX_DIM = 64
SA_SPLITS = (ATT_HEADS * ATT_HEAD_DIM, ATT_HEAD_DIM, ATT_HEAD_DIM, IDX_HEADS * IDX_DIM, IDX_DIM, IDX_HEADS)

VMEM_LIMIT_BYTES = 56 * 1024 * 1024

NT_DIMS = (((1,), (1,)), ((), ()))


def _params(*sem):
    return pltpu.CompilerParams(dimension_semantics=sem, vmem_limit_bytes=VMEM_LIMIT_BYTES)


def _resident(shape):
    nd = len(shape)
    return pl.BlockSpec(shape, lambda *_: (0,) * nd, pipeline_mode=pl.Buffered(1))


def _dot(a, b):
    return jnp.dot(a, b, preferred_element_type=F32)


def _dot_nt(a, b):
    return lax.dot_general(a, b, NT_DIMS, preferred_element_type=F32)


def _split3(a):
    a1 = a.astype(BF16)
    r = a - a1.astype(F32)
    a2 = r.astype(BF16)
    r = r - a2.astype(F32)
    return a1, a2, r.astype(BF16)


def _dot_f32_lhs(a, b01):
    p1, p2, p3 = _split3(a)
    return _dot(p1, b01) + _dot(p2, b01) + _dot(p3, b01)


def _dot_f32_rhs(a01, b):
    p1, p2, p3 = _split3(b)
    return _dot(a01, p1) + _dot(a01, p2) + _dot(a01, p3)


def _rms(x, g):
    return x * lax.rsqrt(jnp.mean(x * x, axis=-1, keepdims=True) + EPS) * g


def _sigmoid(x):
    return 1.0 / (1.0 + jnp.exp(-x))


def _softplus(x):
    return jnp.maximum(x, 0.0) + jnp.log1p(jnp.exp(-jnp.abs(x)))


def _iota(shape, dim):
    return lax.broadcasted_iota(I32, shape, dim)


def _expander(rows, cols, width):
    return (_iota((rows, cols), 1) // width == _iota((rows, cols), 0)).astype(BF16)


def _tril(n):
    return (_iota((n, n), 0) >= _iota((n, n), 1))


def _swiglu_half(x, g_ref, wg_ref, wu_ref, wd_ref, ff_chunk):
    h = _rms(x, g_ref[...]).astype(BF16)
    acc = jnp.zeros(x.shape, F32)
    for c0 in range(0, wg_ref.shape[1], ff_chunk):
        gate = _dot(h, wg_ref[:, c0:c0 + ff_chunk])
        up = _dot(h, wu_ref[:, c0:c0 + ff_chunk])
        act = (gate * _sigmoid(gate) * up).astype(BF16)
        acc = acc + _dot(act, wd_ref[c0:c0 + ff_chunk, :])
    return x + 0.5 * acc


def _ffn_body(x_ref, g_ref, wg_ref, wu_ref, wd_ref, o_ref, *, ff_chunk):
    o_ref[...] = _swiglu_half(x_ref[...], g_ref, wg_ref, wu_ref, wd_ref, ff_chunk)


def _ffn(x2d, g, wg, wu, wd, *, tm, ff_chunk=256):
    t, d = x2d.shape
    return pl.pallas_call(
        functools.partial(_ffn_body, ff_chunk=ff_chunk),
        out_shape=jax.ShapeDtypeStruct((t, d), F32),
        grid=(t // tm,),
        in_specs=[pl.BlockSpec((tm, d), lambda i: (i, 0)), _resident(g.shape),
                  _resident(wg.shape), _resident(wu.shape), _resident(wd.shape)],
        out_specs=pl.BlockSpec((tm, d), lambda i: (i, 0)),
        compiler_params=_params("parallel"),
        name="ffn",
    )(x2d, g, wg, wu, wd)


def _proj_ffn_body(*refs, n_in, ff_chunk):
    x_ref = refs[0]
    a_refs = refs[1:1 + n_in]
    w_refs = refs[1 + n_in:1 + 2 * n_in]
    g_ref, wg_ref, wu_ref, wd_ref, o_ref = refs[1 + 2 * n_in:]
    x = x_ref[...]
    for a_ref, w_ref in zip(a_refs, w_refs):
        x = x + _dot(a_ref[...], w_ref[...])
    o_ref[...] = _swiglu_half(x, g_ref, wg_ref, wu_ref, wd_ref, ff_chunk)


def _proj_ffn(x2d, acts, ws, g, wg, wu, wd, *, tm, ff_chunk=256):
    t, d = x2d.shape
    n_in = len(acts)
    return pl.pallas_call(
        functools.partial(_proj_ffn_body, n_in=n_in, ff_chunk=ff_chunk),
        out_shape=jax.ShapeDtypeStruct((t, d), F32),
        grid=(t // tm,),
        in_specs=([pl.BlockSpec((tm, d), lambda i: (i, 0))]
                  + [pl.BlockSpec((tm, a.shape[1]), lambda i: (i, 0)) for a in acts]
                  + [_resident(w.shape) for w in ws]
                  + [_resident(g.shape), _resident(wg.shape), _resident(wu.shape), _resident(wd.shape)]),
        out_specs=pl.BlockSpec((tm, d), lambda i: (i, 0)),
        compiler_params=_params("parallel"),
        name="proj_ffn",
    )(x2d, *acts, *ws, g, wg, wu, wd)


def _norm_proj_body(x_ref, g_ref, w_ref, *o_refs, widths):
    h = _rms(x_ref[...], g_ref[...]).astype(BF16)
    c0 = 0
    for o_ref, wd in zip(o_refs, widths):
        o_ref[...] = _dot(h, w_ref[:, c0:c0 + wd]).astype(o_ref.dtype)
        c0 += wd


def _norm_proj(x2d, g, w, widths, dtypes, *, tm):
    t, d = x2d.shape
    return pl.pallas_call(
        functools.partial(_norm_proj_body, widths=widths),
        out_shape=[jax.ShapeDtypeStruct((t, wd), dt) for wd, dt in zip(widths, dtypes)],
        grid=(t // tm,),
        in_specs=[pl.BlockSpec((tm, d), lambda i: (i, 0)), _resident(g.shape), _resident(w.shape)],
        out_specs=[pl.BlockSpec((tm, wd), lambda i: (i, 0)) for wd in widths],
        compiler_params=_params("parallel"),
        name="norm_proj",
    )(x2d, g, w)


def _ssd_body(z_ref, xbc_ref, dt_ref, cw_ref, cb_ref, dtb_ref, alog_ref, dskip_ref, nrm_ref, y_ref,
              ubuf, state):
    q_len = CHUNK
    hp = SSD_WIDTH // SSD_GROUPS

    @pl.when(pl.program_id(1) == 0)
    def _():
        ubuf[0:SUBLANES, :] = jnp.zeros((SUBLANES, SSD_CONV_DIM), F32)
        state[...] = jnp.zeros(state.shape, F32)

    u = xbc_ref[0]
    ubuf[SUBLANES:SUBLANES + q_len, :] = u
    conv = cb_ref[...]
    for k in range(CONV_K):
        off = SUBLANES - (CONV_K - 1) + k
        conv = conv + cw_ref[k:k + 1, :] * ubuf[off:off + q_len, :]
    ubuf[0:SUBLANES, :] = u[q_len - SUBLANES:, :]
    xc = conv * _sigmoid(conv)
    xs = xc[:, :SSD_WIDTH]
    bm = xc[:, SSD_WIDTH:SSD_WIDTH + SSD_GROUPS * SSD_STATE]
    cm = xc[:, SSD_WIDTH + SSD_GROUPS * SSD_STATE:]

    lane = _iota((q_len, LANES), 1)
    dt = jnp.where(lane < SSD_HEADS, _softplus(dt_ref[0] + dtb_ref[...]), 0.0)
    a = dt * (-jnp.exp(alog_ref[...]))
    tril = _tril(q_len)
    acs = _dot_f32_rhs(tril.astype(BF16), a)
    eacs = jnp.exp(acs)
    dend = jnp.exp(acs[q_len - 1:q_len, :] - acs)
    expand = _expander(LANES, SSD_WIDTH, SSD_HEAD_DIM)
    wide = _dot_f32_lhs(jnp.concatenate([dt, eacs, dend], axis=0), expand)
    dt_x, eacs_x, dend_x = wide[:q_len], wide[q_len:2 * q_len], wide[2 * q_len:]

    xd = xs * dt_x
    lane_w = _iota((q_len, SSD_WIDTH), 1)
    low_half = (lane_w % LANES) < SSD_HEAD_DIM
    xd_halves = (jnp.where(low_half, xd, 0.0).astype(BF16), jnp.where(low_half, 0.0, xd).astype(BF16))
    xdd_b = (xd * dend_x).astype(BF16)
    acs_t = acs.T
    st = state[...]
    st_b = st.astype(BF16)

    y_diag, y_off, s_new = [], [], []
    heads_per_group = SSD_HEADS // SSD_GROUPS
    for g in range(SSD_GROUPS):
        bg = bm[:, g * SSD_STATE:(g + 1) * SSD_STATE]
        cg_b = cm[:, g * SSD_STATE:(g + 1) * SSD_STATE].astype(BF16)
        cb = _dot_nt(cg_b, bg.astype(BF16))
        s_new.append(_dot(bg.T.astype(BF16), xdd_b[:, g * hp:(g + 1) * hp]))
        y_off.append(_dot(cg_b, st_b[:, g * hp:(g + 1) * hp]))
        for pair in range(heads_per_group // 2):
            acc = None
            for par in range(2):
                h = g * heads_per_group + 2 * pair + par
                seg = acs[:, h:h + 1] - acs_t[h:h + 1, :]
                m = (cb * jnp.exp(jnp.where(tril, seg, -jnp.inf))).astype(BF16)
                slab = (h // 2) * LANES
                d = _dot(m, xd_halves[par][:, slab:slab + LANES])
                acc = d if acc is None else acc + d
            y_diag.append(acc)
    y = (jnp.concatenate(y_diag, axis=1) + eacs_x * jnp.concatenate(y_off, axis=1) + xs * dskip_ref[...])
    state[...] = st * eacs_x[q_len - 1:q_len, :] + jnp.concatenate(s_new, axis=1)

    z = z_ref[0]
    yg = y * (z * _sigmoid(z))
    nrm = nrm_ref[...]
    y_ref[0] = jnp.concatenate(
        [_rms(yg[:, g * hp:(g + 1) * hp], nrm[:, g * hp:(g + 1) * hp]) for g in range(SSD_GROUPS)],
        axis=1).astype(y_ref.dtype)


def _ssd(z, xbc, dts, conv_w, conv_b, dt_bias, a_log, d_skip_x, ssd_norm):
    b, l, _ = z.shape
    nc = l // CHUNK
    blk = lambda w: pl.BlockSpec((1, CHUNK, w), lambda i, c: (i, c, 0))
    return pl.pallas_call(
        _ssd_body,
        out_shape=jax.ShapeDtypeStruct((b, l, SSD_WIDTH), BF16),
        grid=(b, nc),
        in_specs=[blk(SSD_WIDTH), blk(SSD_CONV_DIM), blk(LANES),
                  _resident(conv_w.shape), _resident(conv_b.shape), _resident(dt_bias.shape),
                  _resident(a_log.shape), _resident(d_skip_x.shape), _resident(ssd_norm.shape)],
        out_specs=blk(SSD_WIDTH),
        scratch_shapes=[pltpu.VMEM((SUBLANES + CHUNK, SSD_CONV_DIM), F32),
                        pltpu.VMEM((SSD_STATE, SSD_WIDTH), F32)],
        compiler_params=_params("parallel", "arbitrary"),
        name="ssd",
    )(z, xbc, dts, conv_w, conv_b, dt_bias, a_log, d_skip_x, ssd_norm)


def _mlstm_body(q_ref, k_ref, v_ref, o_ref, i_ref, f_ref, ib_ref, fb_ref, nrm_ref, h_ref, c_st, n_st, m_st):
    q_len = CHUNK

    @pl.when(pl.program_id(1) == 0)
    def _():
        c_st[...] = jnp.zeros(c_st.shape, F32)
        n_st[...] = jnp.zeros(n_st.shape, F32)
        m_st[...] = jnp.zeros(m_st.shape, F32)

    lane = _iota((q_len, LANES), 1)
    is_head = lane < ML_HEADS
    li = jnp.where(is_head, i_ref[0] + ib_ref[...], 0.0)
    lf = jnp.where(is_head, -_softplus(-(f_ref[0] + fb_ref[...])), 0.0)
    tril = _tril(q_len)
    bcs = _dot_f32_rhs(tril.astype(BF16), lf)
    b_last = bcs[q_len - 1:q_len, :]
    gl = b_last - bcs + li
    m_loc = jnp.max(gl, axis=0, keepdims=True)
    wg = jnp.exp(gl - m_loc)
    m_in = m_st[0:1, :]
    m_new = jnp.maximum(b_last + m_in, m_loc)
    a_sc = jnp.exp(b_last + m_in - m_new)
    g_sc = jnp.exp(m_loc - m_new)

    exp_v = _expander(LANES, ML_WIDTH, ML_V_DIM)
    exp_k = _expander(LANES, ML_HEADS * ML_QK_DIM, ML_QK_DIM)
    sc = jnp.concatenate([wg, jnp.broadcast_to(a_sc, (SUBLANES, LANES)),
                          jnp.broadcast_to(g_sc, (SUBLANES, LANES))], axis=0)
    sc_v = _dot_f32_lhs(sc, exp_v)
    sc_k = _dot_f32_lhs(sc, exp_k)
    wg_v, a_v, g_v = sc_v[:q_len], sc_v[q_len:q_len + 1], sc_v[q_len + SUBLANES:q_len + SUBLANES + 1]
    wg_k, a_k, g_k = sc_k[:q_len], sc_k[q_len:q_len + 1], sc_k[q_len + SUBLANES:q_len + SUBLANES + 1]

    q = q_ref[0] * (ML_QK_DIM ** -0.5)
    k = k_ref[0]
    v = v_ref[0]
    q_b, k_b, v_b = q.astype(BF16), k.astype(BF16), v.astype(BF16)
    vw_b = (v * wg_v).astype(BF16)
    n_in = n_st[0:1, :]
    n_loc = jnp.sum(k * wg_k, axis=0, keepdims=True)
    d_t = (li - bcs).T

    outs, c_new = [], []
    for h in range(ML_HEADS):
        ks = slice(h * ML_QK_DIM, (h + 1) * ML_QK_DIM)
        vs = slice(h * ML_V_DIM, (h + 1) * ML_V_DIM)
        c_new.append(_dot(k[:, ks].T.astype(BF16), vw_b[:, vs]))
        dm = jnp.where(tril, bcs[:, h:h + 1] + d_t[h:h + 1, :], -jnp.inf)
        inter = bcs[:, h:h + 1] + m_in[:, h:h + 1]
        m_t = jnp.maximum(jnp.max(dm, axis=1, keepdims=True), inter)
        s = _dot_nt(q_b[:, ks], k_b[:, ks]) * jnp.exp(dm - m_t)
        w_int = jnp.exp(inter - m_t)
        num = _dot(s.astype(BF16), v_b[:, vs]) + w_int * _dot(q_b[:, ks], c_st[:, vs].astype(BF16))
        den = (jnp.sum(s, axis=1, keepdims=True)
               + w_int * jnp.sum(q[:, ks] * n_in[:, ks], axis=1, keepdims=True))
        outs.append(num / jnp.maximum(jnp.abs(den), jnp.exp(-m_t)))
    nrm = nrm_ref[...]
    hn = jnp.concatenate([_rms(outs[h], nrm[:, h * ML_V_DIM:(h + 1) * ML_V_DIM]) for h in range(ML_HEADS)],
                         axis=1)
    h_ref[0] = (hn * _sigmoid(o_ref[0])).astype(h_ref.dtype)

    c_st[...] = a_v * c_st[...] + g_v * jnp.concatenate(c_new, axis=1)
    n_st[...] = jnp.broadcast_to(a_k * n_in + g_k * n_loc, n_st.shape)
    m_st[...] = jnp.broadcast_to(m_new, m_st.shape)


def _mlstm(q, k, v, o, ig, fg, ib, fb, nrm):
    b, l, _ = q.shape
    nc = l // CHUNK
    blk = lambda w: pl.BlockSpec((1, CHUNK, w), lambda i, c: (i, c, 0))
    kw = ML_HEADS * ML_QK_DIM
    return pl.pallas_call(
        _mlstm_body,
        out_shape=jax.ShapeDtypeStruct((b, l, ML_WIDTH), BF16),
        grid=(b, nc),
        in_specs=[blk(kw), blk(kw), blk(ML_WIDTH), blk(ML_WIDTH), blk(LANES), blk(LANES),
                  _resident(ib.shape), _resident(fb.shape), _resident(nrm.shape)],
        out_specs=blk(ML_WIDTH),
        scratch_shapes=[pltpu.VMEM((ML_QK_DIM, ML_WIDTH), F32),
                        pltpu.VMEM((SUBLANES, kw), F32),
                        pltpu.VMEM((SUBLANES, LANES), F32)],
        compiler_params=_params("parallel", "arbitrary"),
        name="mlstm",
    )(q, k, v, o, ig, fg, ib, fb, nrm)


def _rope(x, cos, sin, first_half):
    partner = jnp.where(first_half, pltpu.roll(x, LANES - ATT_HEAD_DIM // 2, 1), pltpu.roll(x, ATT_HEAD_DIM // 2, 1))
    return x * cos + partner * sin


def _sa_proj_body(x_ref, g_ref, w_ref, gq_ref, gk_ref, cos_ref, sin_ref,
                  q_ref, qi_ref, ka_ref, kb_ref, vt_ref, wt_ref):
    tm = x_ref.shape[0]
    tk = vt_ref.shape[2]
    nq = ATT_HEADS * ATT_HEAD_DIM
    nqi = IDX_HEADS * IDX_DIM
    h = _rms(x_ref[...], g_ref[...]).astype(BF16)
    proj = _dot(h, w_ref[...])
    cos, sin = cos_ref[...], sin_ref[...]
    lane = _iota((tm, LANES), 1)
    first_half = (lane % ATT_HEAD_DIM) < ATT_HEAD_DIM // 2
    low_head = lane < ATT_HEAD_DIM

    q = proj[:, :nq]
    seg_sum = (_iota((nq, LANES), 0) // ATT_HEAD_DIM == _iota((nq, LANES), 1)).astype(BF16)
    ms = _dot_f32_lhs(q * q, seg_sum) * (1.0 / ATT_HEAD_DIM)
    rs = _dot_f32_lhs(lax.rsqrt(ms + EPS), _expander(LANES, nq, ATT_HEAD_DIM))
    qn = q * rs * gq_ref[...]
    att_scale = ATT_HEAD_DIM ** -0.5
    for s0 in range(0, nq, LANES):
        q_ref[:, s0:s0 + LANES] = (_rope(qn[:, s0:s0 + LANES], cos, sin, first_half) * att_scale).astype(BF16)
    for s0 in range(0, nqi, LANES):
        qi_ref[:, s0:s0 + LANES] = _rope(proj[:, nq + s0:nq + s0 + LANES], cos, sin, first_half).astype(BF16)

    kk = proj[:, nq + nqi:nq + nqi + LANES]
    k_ms = jnp.sum(jnp.where(low_head, kk * kk, 0.0), axis=1, keepdims=True) * (1.0 / ATT_HEAD_DIM)
    kk = jnp.where(low_head, kk * lax.rsqrt(k_ms + EPS) * gk_ref[...], kk)
    kk = _rope(kk, cos, sin, first_half)
    ka_ref[...] = kk.astype(BF16)
    kb_ref[...] = pltpu.roll(kk, ATT_HEAD_DIM, 1).astype(BF16)
    vv = proj[:, nq + nqi + LANES:nq + nqi + 2 * LANES]
    for c in range(tm // tk):
        vt_ref[c] = vv[c * tk:(c + 1) * tk, :].T[:ATT_HEAD_DIM, :].astype(BF16)
    wt_ref[...] = proj[:, nq + nqi + 2 * LANES:].T[:SUBLANES, :]


def _sa_proj(x2d, g, w, gq, gk, cos, sin, *, tm, tk):
    t, d = x2d.shape
    nq = ATT_HEADS * ATT_HEAD_DIM
    nqi = IDX_HEADS * IDX_DIM
    row = lambda wd: pl.BlockSpec((tm, wd), lambda i: (i, 0))
    outs = [jax.ShapeDtypeStruct((t, nq), BF16), jax.ShapeDtypeStruct((t, nqi), BF16),
            jax.ShapeDtypeStruct((t, LANES), BF16), jax.ShapeDtypeStruct((t, LANES), BF16),
            jax.ShapeDtypeStruct((t // tk, ATT_HEAD_DIM, tk), BF16),
            jax.ShapeDtypeStruct((SUBLANES, t), F32)]
    return pl.pallas_call(
        _sa_proj_body,
        out_shape=outs,
        grid=(t // tm,),
        in_specs=[row(d), _resident(g.shape), _resident(w.shape), _resident(gq.shape), _resident(gk.shape),
                  row(LANES), row(LANES)],
        out_specs=[row(nq), row(nqi), row(LANES), row(LANES),
                   pl.BlockSpec((tm // tk, ATT_HEAD_DIM, tk), lambda i: (i, 0, 0)),
                   pl.BlockSpec((SUBLANES, tm), lambda i: (0, i))],
        compiler_params=_params("parallel"),
        name="sa_proj",
    )(x2d, g, w, gq, gk, cos, sin)


INT_MIN = -2 ** 31
NEG_INF_KEY = INT_MIN + 0x7FFFFF
ATT_KEY_CHUNK = 256


def _sortable_key(x):
    bits = pltpu.bitcast(jnp.where(x == 0.0, 0.0, x), I32)
    return bits ^ ((bits >> 31) & 0x7FFFFFFF)


def _dsa_body(q_ref, qi_ref, wt_ref, ka_ref, kb_ref, vt_ref, o_ref,
              qe, qo, qie, qio, skey, last_s, m_s, l_s, acc_s, *, topk, seq_bits):
    blk = CHUNK
    tk = vt_ref.shape[2]
    j = pl.program_id(1)
    n_chunks = lax.div(j * blk + blk + tk - 1, tk)
    lane = _iota((blk, LANES), 1)
    row = _iota((tk, LANES), 0)
    low = lane < ATT_HEAD_DIM
    q_pos = j * blk + _iota((tk, LANES), 1)

    n_pairs = ATT_HEADS // 2
    for p in range(n_pairs):
        slab = q_ref[0, :, p * LANES:(p + 1) * LANES].astype(F32)
        qe[p * blk:(p + 1) * blk, :] = jnp.where(low, slab, 0.0).astype(BF16)
        qo[p * blk:(p + 1) * blk, :] = jnp.where(low, 0.0, slab).astype(BF16)
    n_ipairs = IDX_HEADS // 2
    for p in range(n_ipairs):
        slab = qi_ref[0, :, p * LANES:(p + 1) * LANES].astype(F32)
        qie[p * blk:(p + 1) * blk, :] = jnp.where(low, slab, 0.0).astype(BF16)
        qio[p * blk:(p + 1) * blk, :] = jnp.where(low, 0.0, slab).astype(BF16)

    idx_scale = (IDX_HEADS ** -0.5) * (IDX_DIM ** -0.5)
    wt = wt_ref[...]

    def chunk(ci):
        return pl.ds(pl.multiple_of(ci * tk, tk), tk)

    def score_chunk(ci, _):
        le = _dot_nt(kb_ref[0, chunk(ci), :], qie[...])
        lo = _dot_nt(ka_ref[0, chunk(ci), :], qio[...])
        sc = jnp.zeros((tk, LANES), F32)
        for p in range(n_ipairs):
            sc = sc + jnp.maximum(le[:, p * blk:(p + 1) * blk], 0.0) * wt[2 * p:2 * p + 1, :]
            sc = sc + jnp.maximum(lo[:, p * blk:(p + 1) * blk], 0.0) * wt[2 * p + 1:2 * p + 2, :]
        sc = sc * idx_scale
        sc = jnp.where(ci * tk + row <= q_pos, sc, -jnp.inf)
        skey[chunk(ci), :] = _sortable_key(sc)
        return 0

    lax.fori_loop(0, n_chunks, score_chunk, 0)

    acc_rows = 4 * SUBLANES

    def count(pred_fn):
        def body(ci, c):
            hit = jnp.where(pred_fn(skey[chunk(ci), :], ci), 1.0, 0.0)
            return c + jnp.sum(hit.reshape(tk // acc_rows, acc_rows, LANES), axis=0)
        part = lax.fori_loop(0, n_chunks, body, jnp.zeros((acc_rows, LANES), F32))
        return jnp.sum(part, axis=0, keepdims=True)

    kf = float(topk)
    thr = jnp.full((1, LANES), INT_MIN, I32)
    n_ge = jnp.full((1, LANES), 1.0, F32) * (n_chunks * tk).astype(F32)
    for bit in range(31, -1, -1):
        cand = thr + jnp.int32(INT_MIN if bit == 31 else 1 << bit)
        c = count(lambda key, ci, cand=cand: key >= cand)
        thr = jnp.where(c >= kf, cand, thr)
        n_ge = jnp.where(c >= kf, c, n_ge)

    last_s[...] = jnp.full(last_s.shape, 2 ** 31 - 1, I32)
    tied = (n_ge > kf) & (thr > NEG_INF_KEY)

    @pl.when(jnp.sum(jnp.where(tied, 1.0, 0.0)) > 0.0)
    def _():
        need = kf - count(lambda key, ci: key > thr)
        last = jnp.zeros((1, LANES), I32)
        for bit in range(seq_bits - 1, -1, -1):
            cand = last + jnp.int32(1 << bit)
            c = count(lambda key, ci, cand=cand: (key == thr) & (ci * tk + row < cand))
            last = jnp.where(c <= need - 1.0, cand, last)
        last_s[...] = jnp.broadcast_to(last, last_s.shape)

    last = last_s[0:1, :]

    m_s[...] = jnp.full(m_s.shape, -1e30, F32)
    l_s[...] = jnp.zeros(l_s.shape, F32)
    acc_s[...] = jnp.zeros(acc_s.shape, F32)

    def attn_chunk(ci, _):
        key = skey[chunk(ci), :]
        pos = ci * tk + row
        sel = ((key > thr) | ((key == thr) & (pos <= last))) & (pos <= q_pos)
        bias = jnp.where(sel, 0.0, -jnp.inf)
        s = jnp.concatenate([_dot_nt(ka_ref[0, chunk(ci), :], qe[...]),
                             _dot_nt(kb_ref[0, chunk(ci), :], qo[...])], axis=1)
        s = s + jnp.concatenate([bias] * ATT_HEADS, axis=1)
        m_prev = m_s[...]
        m_cur = jnp.maximum(m_prev, jnp.max(s, axis=0, keepdims=True))
        alpha = jnp.exp(m_prev - m_cur)
        p = jnp.exp(s - m_cur)
        l_s[...] = alpha * l_s[...] + jnp.sum(p, axis=0, keepdims=True)
        acc_s[...] = alpha * acc_s[...] + _dot(vt_ref[ci], p.astype(BF16))
        m_s[...] = m_cur
        return 0

    lax.fori_loop(0, n_chunks, attn_chunk, 0)

    out = acc_s[...] / l_s[...]
    half = n_pairs * blk
    for p in range(n_pairs):
        pair = jnp.concatenate([out[:, p * blk:(p + 1) * blk], out[:, half + p * blk:half + (p + 1) * blk]], axis=0)
        o_ref[0, :, p * LANES:(p + 1) * LANES] = pair.T.astype(o_ref.dtype)


def _dsa(q, qi, wt, ka, kb, vt, *, topk):
    b, l, nq = q.shape
    nqi = qi.shape[2]
    nb = l // CHUNK
    tk = vt.shape[2]
    rows = (ATT_HEADS // 2) * CHUNK
    irows = (IDX_HEADS // 2) * CHUNK
    qblk = lambda w: pl.BlockSpec((1, CHUNK, w), lambda i, j: (i, j, 0))
    seq = pl.BlockSpec((1, l, LANES), lambda i, j: (i, 0, 0))
    return pl.pallas_call(
        functools.partial(_dsa_body, topk=topk, seq_bits=int(np.log2(l))),
        out_shape=jax.ShapeDtypeStruct((b, l, nq), BF16),
        grid=(b, nb),
        in_specs=[qblk(nq), qblk(nqi), pl.BlockSpec((SUBLANES, CHUNK), lambda i, j: (0, i * nb + j)),
                  seq, seq, pl.BlockSpec((l // tk, ATT_HEAD_DIM, tk), lambda i, j: (i, 0, 0))],
        out_specs=qblk(nq),
        scratch_shapes=[pltpu.VMEM((rows, LANES), BF16), pltpu.VMEM((rows, LANES), BF16),
                        pltpu.VMEM((irows, LANES), BF16), pltpu.VMEM((irows, LANES), BF16),
                        pltpu.VMEM((l, LANES), I32), pltpu.VMEM((SUBLANES, LANES), I32),
                        pltpu.VMEM((1, 2 * rows), F32), pltpu.VMEM((1, 2 * rows), F32),
                        pltpu.VMEM((ATT_HEAD_DIM, 2 * rows), F32)],
        compiler_params=_params("parallel", "arbitrary"),
        name="dsa",
    )(q, qi, wt, ka, kb, vt)


def _split_cols(w, sizes):
    out, c0 = [], 0
    for s in sizes:
        out.append(w[:, c0:c0 + s])
        c0 += s
    return out


def _pad_cols(w, width):
    return jnp.pad(w, ((0, 0), (0, width - w.shape[1])))


def _pad_row(v, width):
    return jnp.pad(v, (0, width - v.shape[0]))[None, :]


def _hybrid_mixer_parts(x2d, b, l, mix_norm, w_in, conv_w, conv_b, dt_bias, a_log, d_skip, ssd_norm,
                        igate_bias, fgate_bias, mlstm_norm, *, tm):
    wz, wxbc, wdt, wq, wk, wv, wo, wi, wf = _split_cols(w_in, HY_SPLITS)
    w = jnp.concatenate([wz, wxbc, wq, wk, wv, wo, _pad_cols(wdt, LANES), _pad_cols(wi, LANES),
                         _pad_cols(wf, LANES)], axis=1).astype(BF16)
    kw = ML_HEADS * ML_QK_DIM
    widths = (SSD_WIDTH, SSD_CONV_DIM, kw, kw, ML_WIDTH, ML_WIDTH, LANES, LANES, LANES)
    z, xbc, qm, km, vm, om, dts, ig, fg = _norm_proj(x2d, mix_norm[None, :], w, widths, (F32,) * len(widths), tm=tm)
    r3 = lambda a: a.reshape(b, l, a.shape[1])
    y = _ssd(r3(z), r3(xbc), r3(dts), conv_w, conv_b[None, :], _pad_row(dt_bias, LANES), _pad_row(a_log, LANES),
             jnp.repeat(d_skip, SSD_HEAD_DIM)[None, :], ssd_norm[None, :])
    hm = _mlstm(r3(qm), r3(km), r3(vm), r3(om), r3(ig), r3(fg), _pad_row(igate_bias, LANES),
                _pad_row(fgate_bias, LANES), mlstm_norm[None, :])
    return y.reshape(b * l, SSD_WIDTH), hm.reshape(b * l, ML_WIDTH)


def _rope_tables(positions):
    half = ATT_HEAD_DIM // 2
    inv = ROPE_THETA ** (-jnp.arange(0, ATT_HEAD_DIM, 2, dtype=F32) / ATT_HEAD_DIM)
    ang = positions.astype(F32)[..., None] * inv
    cos, sin = jnp.cos(ang), jnp.sin(ang)
    reps = LANES // half
    cos_t = jnp.tile(cos, (1, 1, reps))
    sin_t = jnp.tile(jnp.concatenate([-sin, sin], axis=-1), (1, 1, reps // 2))
    t = positions.shape[0] * positions.shape[1]
    return cos_t.reshape(t, LANES), sin_t.reshape(t, LANES)


def _sparse_attention_parts(x2d, b, l, cos, sin, mix_norm, w_in, q_norm, k_norm, topk, *, tm):
    wq, wk, wv, wqi, wki, wwi = _split_cols(w_in, SA_SPLITS)
    w = jnp.concatenate([wq, wqi, wk, wki, _pad_cols(wv, LANES), _pad_cols(wwi, LANES)], axis=1).astype(BF16)
    gq = jnp.tile(q_norm, ATT_HEADS)[None, :]
    gk = _pad_row(k_norm, LANES)
    q, qi, ka, kb, vt, wt = _sa_proj(x2d, mix_norm[None, :], w, gq, gk, cos, sin, tm=tm, tk=ATT_KEY_CHUNK)
    r3 = lambda a: a.reshape(b, l, a.shape[1])
    o = _dsa(r3(q), r3(qi), wt, r3(ka), r3(kb), vt, topk=topk)
    return o.reshape(b * l, ATT_HEADS * ATT_HEAD_DIM)


def kernel(x, positions, ffn_norm, ffn_w_gate, ffn_w_up, ffn_w_down, mix_norm, hy_w_in, hy_conv_w, hy_conv_b,
           hy_dt_bias, hy_a_log, hy_d_skip, hy_ssd_norm, hy_igate_bias, hy_fgate_bias, hy_mlstm_norm, hy_w_out,
           sa_w_in, sa_q_norm, sa_k_norm, sa_w_out):
    b, l, d = x.shape
    depth = ffn_norm.shape[0]
    t = b * l
    tm = 512 if t % 512 == 0 else CHUNK
    topk = min(TOPK_MAX, l // 4)
    cos, sin = _rope_tables(positions)
    wg, wu, wd = ffn_w_gate.astype(BF16), ffn_w_up.astype(BF16), ffn_w_down.astype(BF16)
    x2d = x.reshape(t, d)
    for layer in range(depth):
        x2d = _ffn(x2d, ffn_norm[layer, 0][None, :], wg[layer, 0], wu[layer, 0], wd[layer, 0], tm=tm)
        if layer % 2 == 0:
            e = layer // 2
            y, hm = _hybrid_mixer_parts(x2d, b, l, mix_norm[layer], hy_w_in[e], hy_conv_w[e], hy_conv_b[e],
                                        hy_dt_bias[e], hy_a_log[e], hy_d_skip[e], hy_ssd_norm[e],
                                        hy_igate_bias[e], hy_fgate_bias[e], hy_mlstm_norm[e], tm=tm)
            w_out = hy_w_out[e].astype(BF16)
            acts, ws = [y, hm], [w_out[:SSD_WIDTH], w_out[SSD_WIDTH:]]
        else:
            o = layer // 2
            att = _sparse_attention_parts(x2d, b, l, cos, sin, mix_norm[layer], sa_w_in[o], sa_q_norm[o],
                                          sa_k_norm[o], topk, tm=tm)
            acts, ws = [att], [sa_w_out[o].astype(BF16)]
        x2d = _proj_ffn(x2d, acts, ws, ffn_norm[layer, 1][None, :], wg[layer, 1], wu[layer, 1], wd[layer, 1], tm=tm)
    return x2d.reshape(b, l, d)
```

```python
import functools

import jax
import jax.numpy as jnp
import numpy as np
from jax import lax
from jax.experimental import pallas as pl
from jax.experimental.pallas import tpu as pltpu

F32 = jnp.float32
BF16 = jnp.bfloat16
I32 = jnp.int32

EPS = 1e-6
ROPE_THETA = 10000.0
LOG2_E = 1.4426950408889634
LANES = 128
SUBLANES = 8
CHUNK = 128
CONV_K = 4
TOPK_MAX = 256

D_MODEL = 1024
D_FF = 2816
SSD_WIDTH = 1024
SSD_HEAD_DIM = 64
SSD_HEADS = 16
SSD_GROUPS = 2
SSD_STATE = 128
SSD_CONV_DIM = SSD_WIDTH + 2 * SSD_GROUPS * SSD_STATE
ML_WIDTH = 1024
ML_HEADS = 4
ML_V_DIM = 256
ML_QK_DIM = 128
HY_SPLITS = (SSD_WIDTH, SSD_CONV_DIM, SSD_HEADS, ML_HEADS * ML_QK_DIM, ML_HEADS * ML_QK_DIM,
             ML_WIDTH, ML_WIDTH, ML_HEADS, ML_HEADS)
ATT_HEAD_DIM = 64
ATT_HEADS = 16
IDX_HEADS = 8
IDX_DIM = 64
SA_SPLITS = (ATT_HEADS * ATT_HEAD_DIM, ATT_HEAD_DIM, ATT_HEAD_DIM, IDX_HEADS * IDX_DIM, IDX_DIM, IDX_HEADS)

VMEM_LIMIT_BYTES = 56 * 1024 * 1024

NT_DIMS = (((1,), (1,)), ((), ()))


def _params(*sem):
    return pltpu.CompilerParams(dimension_semantics=sem, vmem_limit_bytes=VMEM_LIMIT_BYTES)


def _resident(shape):
    nd = len(shape)
    return pl.BlockSpec(shape, lambda *_: (0,) * nd, pipeline_mode=pl.Buffered(1))


def _dot(a, b):
    return jnp.dot(a, b, preferred_element_type=F32)


def _dot_nt(a, b):
    return lax.dot_general(a, b, NT_DIMS, preferred_element_type=F32)


def _split3(a):
    a1 = a.astype(BF16)
    r = a - a1.astype(F32)
    a2 = r.astype(BF16)
    r = r - a2.astype(F32)
    return a1, a2, r.astype(BF16)


def _dot_f32_lhs(a, b01):
    p1, p2, p3 = _split3(a)
    return _dot(p1, b01) + _dot(p2, b01) + _dot(p3, b01)


def _dot_f32_rhs(a01, b):
    p1, p2, p3 = _split3(b)
    return _dot(a01, p1) + _dot(a01, p2) + _dot(a01, p3)


def _rms(x, g):
    return x * lax.rsqrt(jnp.mean(x * x, axis=-1, keepdims=True) + EPS) * g


def _sigmoid(x):
    return 1.0 / (1.0 + jnp.exp(-x))


def _softplus(x):
    return jnp.maximum(x, 0.0) + jnp.log1p(jnp.exp(-jnp.abs(x)))


def _iota(shape, dim):
    return lax.broadcasted_iota(I32, shape, dim)


def _expander(rows, cols, width):
    e = np.arange(cols)[None, :] // width == np.arange(rows)[:, None]
    return jnp.asarray(e, dtype=BF16)


def _stacked(w, layer, idx):
    return pl.BlockSpec((None, None) + w.shape[2:], lambda *_: (layer, idx, 0, 0), pipeline_mode=pl.Buffered(1))


def _tril(n):
    return (_iota((n, n), 0) >= _iota((n, n), 1))


def _swiglu_half(x, g_ref, wg_ref, wu_ref, wd_ref, ff_chunk):
    h = _rms(x, g_ref[...]).astype(BF16)
    acc = jnp.zeros(x.shape, F32)
    for c0 in range(0, wg_ref.shape[1], ff_chunk):
        gate = _dot(h, wg_ref[:, c0:c0 + ff_chunk])
        up = _dot(h, wu_ref[:, c0:c0 + ff_chunk])
        act = (gate * _sigmoid(gate) * up).astype(BF16)
        acc = acc + _dot(act, wd_ref[c0:c0 + ff_chunk, :])
    return x + 0.5 * acc


def _ffn_body(x_ref, g_ref, wg_ref, wu_ref, wd_ref, o_ref, *, ff_chunk):
    o_ref[...] = _swiglu_half(x_ref[...], g_ref, wg_ref, wu_ref, wd_ref, ff_chunk)


def _ffn(x2d, g, wg, wu, wd, layer, idx, *, tm, ff_chunk=256):
    t, d = x2d.shape
    return pl.pallas_call(
        functools.partial(_ffn_body, ff_chunk=ff_chunk),
        out_shape=jax.ShapeDtypeStruct((t, d), F32),
        grid=(t // tm,),
        in_specs=[pl.BlockSpec((tm, d), lambda i: (i, 0)), _resident(g.shape),
                  _stacked(wg, layer, idx), _stacked(wu, layer, idx), _stacked(wd, layer, idx)],
        out_specs=pl.BlockSpec((tm, d), lambda i: (i, 0)),
        compiler_params=_params("parallel"),
        name="ffn",
    )(x2d, g, wg, wu, wd)


def _proj_ffn_body(*refs, n_in, ff_chunk):
    x_ref = refs[0]
    a_refs = refs[1:1 + n_in]
    w_refs = refs[1 + n_in:1 + 2 * n_in]
    g_ref, wg_ref, wu_ref, wd_ref, o_ref = refs[1 + 2 * n_in:]
    x = x_ref[...]
    for a_ref, w_ref in zip(a_refs, w_refs):
        x = x + _dot(a_ref[...], w_ref[...])
    o_ref[...] = _swiglu_half(x, g_ref, wg_ref, wu_ref, wd_ref, ff_chunk)


def _proj_ffn(x2d, acts, ws, g, wg, wu, wd, layer, idx, *, tm, ff_chunk=256):
    t, d = x2d.shape
    n_in = len(acts)
    return pl.pallas_call(
        functools.partial(_proj_ffn_body, n_in=n_in, ff_chunk=ff_chunk),
        out_shape=jax.ShapeDtypeStruct((t, d), F32),
        grid=(t // tm,),
        in_specs=([pl.BlockSpec((tm, d), lambda i: (i, 0))]
                  + [pl.BlockSpec((tm, a.shape[1]), lambda i: (i, 0)) for a in acts]
                  + [_resident(w.shape) for w in ws]
                  + [_resident(g.shape), _stacked(wg, layer, idx), _stacked(wu, layer, idx),
                     _stacked(wd, layer, idx)]),
        out_specs=pl.BlockSpec((tm, d), lambda i: (i, 0)),
        compiler_params=_params("parallel"),
        name="proj_ffn",
    )(x2d, *acts, *ws, g, wg, wu, wd)


def _hy_proj_body(x_ref, g_ref, w_ref, cw_ref, cb_ref, z_ref, xc_ref, q_ref, k_ref, v_ref, o_ref, dt_ref, i_ref,
                  f_ref, ubuf, *, tiles_per_seq):
    tm = x_ref.shape[0]

    @pl.when(pl.program_id(0) % tiles_per_seq == 0)
    def _():
        ubuf[0:SUBLANES, :] = jnp.zeros((SUBLANES, SSD_CONV_DIM), F32)

    h = _rms(x_ref[...], g_ref[...]).astype(BF16)

    u = _dot(h, w_ref[:, SSD_WIDTH:SSD_WIDTH + SSD_CONV_DIM])
    ubuf[SUBLANES:SUBLANES + tm, :] = u
    conv = cb_ref[...] + cw_ref[CONV_K - 1:CONV_K, :] * u
    for k in range(CONV_K - 1):
        off = SUBLANES - (CONV_K - 1) + k
        conv = conv + cw_ref[k:k + 1, :] * ubuf[off:off + tm, :]
    ubuf[0:SUBLANES, :] = u[tm - SUBLANES:, :]
    xc_ref[...] = (conv * _sigmoid(conv)).astype(xc_ref.dtype)

    c0 = 0
    for out_ref in (z_ref, None, q_ref, k_ref, v_ref, o_ref, dt_ref, i_ref, f_ref):
        if out_ref is None:
            c0 += SSD_CONV_DIM
            continue
        wd = out_ref.shape[1]
        out_ref[...] = _dot(h, w_ref[:, c0:c0 + wd]).astype(out_ref.dtype)
        c0 += wd


def _hy_proj(x2d, g, w, conv_w, conv_b, seq_len, *, tm):
    t, d = x2d.shape
    kw = ML_HEADS * ML_QK_DIM
    widths = (SSD_WIDTH, SSD_CONV_DIM, kw, kw, ML_WIDTH, ML_WIDTH, LANES, LANES, LANES)
    return pl.pallas_call(
        functools.partial(_hy_proj_body, tiles_per_seq=seq_len // tm),
        out_shape=[jax.ShapeDtypeStruct((t, wd), F32) for wd in widths],
        grid=(t // tm,),
        in_specs=[pl.BlockSpec((tm, d), lambda i: (i, 0)), _resident(g.shape), _resident(w.shape),
                  _resident(conv_w.shape), _resident(conv_b.shape)],
        out_specs=[pl.BlockSpec((tm, wd), lambda i: (i, 0)) for wd in widths],
        scratch_shapes=[pltpu.VMEM((SUBLANES + tm, SSD_CONV_DIM), F32)],
        compiler_params=_params("arbitrary"),
        name="hy_proj",
    )(x2d, g, w, conv_w, conv_b)


def _ssd_body(z_ref, xc_ref, dt_ref, dtb_ref, alog_ref, dskip_ref, nrm_ref, tril_ref, expand_ref, y_ref, state):
    q_len = CHUNK
    hp = SSD_WIDTH // SSD_GROUPS
    xc = xc_ref[0]
    xs = xc[:, :SSD_WIDTH]
    bm = xc[:, SSD_WIDTH:SSD_WIDTH + SSD_GROUPS * SSD_STATE]
    cm = xc[:, SSD_WIDTH + SSD_GROUPS * SSD_STATE:]

    lane = _iota((q_len, LANES), 1)
    dt = jnp.where(lane < SSD_HEADS, _softplus(dt_ref[0] + dtb_ref[...]), 0.0)
    a = dt * (-jnp.exp(alog_ref[...]))
    tril = _tril(q_len)
    acs = _dot_f32_rhs(tril_ref[...], a)
    eacs = jnp.exp(acs)
    dend = jnp.exp(acs[q_len - 1:q_len, :] - acs)
    wide = _dot_f32_lhs(jnp.concatenate([dt, eacs, dend], axis=0), expand_ref[...])
    dt_x, eacs_x, dend_x = wide[:q_len], wide[q_len:2 * q_len], wide[2 * q_len:]

    xd = xs * dt_x
    lane_w = _iota((q_len, SSD_WIDTH), 1)
    low_half = (lane_w % LANES) < SSD_HEAD_DIM
    xd_halves = (jnp.where(low_half, xd, 0.0).astype(BF16), jnp.where(low_half, 0.0, xd).astype(BF16))
    xdd_b = (xd * dend_x).astype(BF16)
    acs_t = acs.T
    st = state[...]
    st_b = st.astype(BF16)

    y_diag, y_off, s_new = [], [], []
    heads_per_group = SSD_HEADS // SSD_GROUPS
    for g in range(SSD_GROUPS):
        bg = bm[:, g * SSD_STATE:(g + 1) * SSD_STATE]
        cg_b = cm[:, g * SSD_STATE:(g + 1) * SSD_STATE].astype(BF16)
        cb = _dot_nt(cg_b, bg.astype(BF16))
        s_new.append(_dot(bg.T.astype(BF16), xdd_b[:, g * hp:(g + 1) * hp]))
        y_off.append(_dot(cg_b, st_b[:, g * hp:(g + 1) * hp]))
        for pair in range(heads_per_group // 2):
            acc = None
            for par in range(2):
                h = g * heads_per_group + 2 * pair + par
                seg = acs[:, h:h + 1] - acs_t[h:h + 1, :]
                m = (cb * jnp.exp(jnp.where(tril, seg, -jnp.inf))).astype(BF16)
                slab = (h // 2) * LANES
                d = _dot(m, xd_halves[par][:, slab:slab + LANES])
                acc = d if acc is None else acc + d
            y_diag.append(acc)
    y = (jnp.concatenate(y_diag, axis=1) + eacs_x * jnp.concatenate(y_off, axis=1) + xs * dskip_ref[...])
    state[...] = st * eacs_x[q_len - 1:q_len, :] + jnp.concatenate(s_new, axis=1)

    z = z_ref[0]
    yg = y * (z * _sigmoid(z))
    nrm = nrm_ref[...]
    y_ref[0] = jnp.concatenate(
        [_rms(yg[:, g * hp:(g + 1) * hp], nrm[:, g * hp:(g + 1) * hp]) for g in range(SSD_GROUPS)],
        axis=1).astype(y_ref.dtype)


def _mlstm_body(q_ref, k_ref, v_ref, o_ref, i_ref, f_ref, ib_ref, fb_ref, nrm_ref, tril_ref, exp_v_ref, exp_k_ref,
                h_ref, c_st, n_st, m_st):
    q_len = CHUNK

    lane = _iota((q_len, LANES), 1)
    is_head = lane < ML_HEADS
    li = jnp.where(is_head, i_ref[0] + ib_ref[...], 0.0)
    lf = jnp.where(is_head, -_softplus(-(f_ref[0] + fb_ref[...])), 0.0)
    tril = _tril(q_len)
    bcs = _dot_f32_rhs(tril_ref[...], lf)
    b_last = bcs[q_len - 1:q_len, :]
    gl = b_last - bcs + li
    m_loc = jnp.max(gl, axis=0, keepdims=True)
    wg = jnp.exp(gl - m_loc)
    m_in = m_st[0:1, :]
    m_new = jnp.maximum(b_last + m_in, m_loc)
    a_sc = jnp.exp(b_last + m_in - m_new)
    g_sc = jnp.exp(m_loc - m_new)

    sc = jnp.concatenate([wg, jnp.broadcast_to(a_sc, (SUBLANES, LANES)),
                          jnp.broadcast_to(g_sc, (SUBLANES, LANES))], axis=0)
    sc_v = _dot_f32_lhs(sc, exp_v_ref[...])
    sc_k = _dot_f32_lhs(sc, exp_k_ref[...])
    wg_v, a_v, g_v = sc_v[:q_len], sc_v[q_len:q_len + 1], sc_v[q_len + SUBLANES:q_len + SUBLANES + 1]
    wg_k, a_k, g_k = sc_k[:q_len], sc_k[q_len:q_len + 1], sc_k[q_len + SUBLANES:q_len + SUBLANES + 1]

    q = q_ref[0] * (ML_QK_DIM ** -0.5)
    k = k_ref[0]
    v = v_ref[0]
    q_b, k_b, v_b = q.astype(BF16), k.astype(BF16), v.astype(BF16)
    vw_b = (v * wg_v).astype(BF16)
    n_in = n_st[0:1, :]
    n_loc = jnp.sum(k * wg_k, axis=0, keepdims=True)
    d_t = (li - bcs).T

    outs, c_new = [], []
    for h in range(ML_HEADS):
        ks = slice(h * ML_QK_DIM, (h + 1) * ML_QK_DIM)
        vs = slice(h * ML_V_DIM, (h + 1) * ML_V_DIM)
        c_new.append(_dot(k[:, ks].T.astype(BF16), vw_b[:, vs]))
        dm = jnp.where(tril, bcs[:, h:h + 1] + d_t[h:h + 1, :], -jnp.inf)
        inter = bcs[:, h:h + 1] + m_in[:, h:h + 1]
        m_t = jnp.maximum(jnp.max(dm, axis=1, keepdims=True), inter)
        s = _dot_nt(q_b[:, ks], k_b[:, ks]) * jnp.exp(dm - m_t)
        w_int = jnp.exp(inter - m_t)
        num = _dot(s.astype(BF16), v_b[:, vs]) + w_int * _dot(q_b[:, ks], c_st[:, vs].astype(BF16))
        den = (jnp.sum(s, axis=1, keepdims=True)
               + w_int * jnp.sum(q[:, ks] * n_in[:, ks], axis=1, keepdims=True))
        outs.append(num / jnp.maximum(jnp.abs(den), jnp.exp(-m_t)))
    nrm = nrm_ref[...]
    hn = jnp.concatenate([_rms(outs[h], nrm[:, h * ML_V_DIM:(h + 1) * ML_V_DIM]) for h in range(ML_HEADS)],
                         axis=1)
    h_ref[0] = (hn * _sigmoid(o_ref[0])).astype(h_ref.dtype)

    c_st[...] = a_v * c_st[...] + g_v * jnp.concatenate(c_new, axis=1)
    n_st[...] = jnp.broadcast_to(a_k * n_in + g_k * n_loc, n_st.shape)
    m_st[...] = jnp.broadcast_to(m_new, m_st.shape)


MIXER_SEQS_PER_STEP = 4


def _mixer_body(z_ref, xc_ref, dt_ref, q_ref, k_ref, v_ref, o_ref, i_ref, f_ref,
                dtb_ref, alog_ref, dskip_ref, snrm_ref, ib_ref, fb_ref, mnrm_ref,
                tril_ref, exp_h_ref, exp_v_ref, exp_k_ref, y_ref, h_ref, state, c_st, n_st, m_st):
    @pl.when(pl.program_id(1) == 0)
    def _():
        for st_ref in (state, c_st, n_st, m_st):
            st_ref[...] = jnp.zeros(st_ref.shape, F32)

    for bi in range(z_ref.shape[0]):
        one = lambda ref: ref.at[bi:bi + 1]
        _ssd_body(one(z_ref), one(xc_ref), one(dt_ref), dtb_ref, alog_ref, dskip_ref, snrm_ref, tril_ref, exp_h_ref,
                  one(y_ref), state.at[bi])
        _mlstm_body(one(q_ref), one(k_ref), one(v_ref), one(o_ref), one(i_ref), one(f_ref), ib_ref, fb_ref, mnrm_ref,
                    tril_ref, exp_v_ref, exp_k_ref, one(h_ref), c_st.at[bi], n_st.at[bi], m_st.at[bi])


def _mixer(acts, params):
    b, l, _ = acts[0].shape
    kw = ML_HEADS * ML_QK_DIM
    consts = [jnp.asarray(np.tril(np.ones((CHUNK, CHUNK))), dtype=BF16),
              _expander(LANES, SSD_WIDTH, SSD_HEAD_DIM), _expander(LANES, ML_WIDTH, ML_V_DIM),
              _expander(LANES, kw, ML_QK_DIM)]
    nb = MIXER_SEQS_PER_STEP if b % MIXER_SEQS_PER_STEP == 0 else 1
    blk = lambda w: pl.BlockSpec((nb, CHUNK, w), lambda i, c: (i, c, 0))
    return pl.pallas_call(
        _mixer_body,
        out_shape=[jax.ShapeDtypeStruct((b, l, SSD_WIDTH), BF16), jax.ShapeDtypeStruct((b, l, ML_WIDTH), BF16)],
        grid=(b // nb, l // CHUNK),
        in_specs=([blk(a.shape[2]) for a in acts] + [_resident(p.shape) for p in params]
                  + [_resident(c.shape) for c in consts]),
        out_specs=[blk(SSD_WIDTH), blk(ML_WIDTH)],
        scratch_shapes=[pltpu.VMEM((nb, SSD_STATE, SSD_WIDTH), F32),
                        pltpu.VMEM((nb, ML_QK_DIM, ML_WIDTH), F32),
                        pltpu.VMEM((nb, SUBLANES, kw), F32),
                        pltpu.VMEM((nb, SUBLANES, LANES), F32)],
        compiler_params=_params("parallel", "arbitrary"),
        name="mixer",
    )(*acts, *params, *consts)


def _rope(x, cos, sin, first_half):
    partner = jnp.where(first_half, pltpu.roll(x, LANES - ATT_HEAD_DIM // 2, 1), pltpu.roll(x, ATT_HEAD_DIM // 2, 1))
    return x * cos + partner * sin


def _sa_proj_body(x_ref, g_ref, w_ref, gq_ref, gk_ref, cos_ref, sin_ref, seg_ref, expand_ref,
                  q_ref, qi_ref, ka_ref, kb_ref, vt_ref, wt_ref):
    tm = x_ref.shape[0]
    tk = vt_ref.shape[2]
    nq = ATT_HEADS * ATT_HEAD_DIM
    nqi = IDX_HEADS * IDX_DIM
    h = _rms(x_ref[...], g_ref[...]).astype(BF16)
    proj = _dot(h, w_ref[...])
    cos, sin = cos_ref[...], sin_ref[...]
    lane = _iota((tm, LANES), 1)
    first_half = (lane % ATT_HEAD_DIM) < ATT_HEAD_DIM // 2
    low_head = lane < ATT_HEAD_DIM

    q = proj[:, :nq]
    ms = _dot_f32_lhs(q * q, seg_ref[...]) * (1.0 / ATT_HEAD_DIM)
    rs = _dot_f32_lhs(lax.rsqrt(ms + EPS), expand_ref[...])
    qn = q * rs * gq_ref[...]
    q_scale = ATT_HEAD_DIM ** -0.5 * LOG2_E
    for s0 in range(0, nq, LANES):
        q_ref[:, s0:s0 + LANES] = (_rope(qn[:, s0:s0 + LANES], cos, sin, first_half) * q_scale).astype(BF16)
    for s0 in range(0, nqi, LANES):
        qi_ref[:, s0:s0 + LANES] = _rope(proj[:, nq + s0:nq + s0 + LANES], cos, sin, first_half).astype(BF16)

    kk = proj[:, nq + nqi:nq + nqi + LANES]
    k_ms = jnp.sum(jnp.where(low_head, kk * kk, 0.0), axis=1, keepdims=True) * (1.0 / ATT_HEAD_DIM)
    kk = jnp.where(low_head, kk * lax.rsqrt(k_ms + EPS) * gk_ref[...], kk)
    kk = _rope(kk, cos, sin, first_half)
    ka_ref[...] = kk.astype(BF16)
    kb_ref[...] = pltpu.roll(kk, ATT_HEAD_DIM, 1).astype(BF16)
    vv = proj[:, nq + nqi + LANES:nq + nqi + 2 * LANES]
    for c in range(tm // tk):
        vt_ref[c] = vv[c * tk:(c + 1) * tk, :].T[:ATT_HEAD_DIM, :].astype(BF16)
    wt_ref[...] = proj[:, nq + nqi + 2 * LANES:].T[:SUBLANES, :]


def _sa_proj(x2d, g, w, gq, gk, cos, sin, *, tm, tk):
    t, d = x2d.shape
    nq = ATT_HEADS * ATT_HEAD_DIM
    nqi = IDX_HEADS * IDX_DIM
    row = lambda wd: pl.BlockSpec((tm, wd), lambda i: (i, 0))
    outs = [jax.ShapeDtypeStruct((t, nq), BF16), jax.ShapeDtypeStruct((t, nqi), BF16),
            jax.ShapeDtypeStruct((t, LANES), BF16), jax.ShapeDtypeStruct((t, LANES), BF16),
            jax.ShapeDtypeStruct((t // tk, ATT_HEAD_DIM, tk), BF16),
            jax.ShapeDtypeStruct((SUBLANES, t), F32)]
    expand = _expander(LANES, nq, ATT_HEAD_DIM)
    seg = expand.T
    return pl.pallas_call(
        _sa_proj_body,
        out_shape=outs,
        grid=(t // tm,),
        in_specs=[row(d), _resident(g.shape), _resident(w.shape), _resident(gq.shape), _resident(gk.shape),
                  row(LANES), row(LANES), _resident(seg.shape), _resident(expand.shape)],
        out_specs=[row(nq), row(nqi), row(LANES), row(LANES),
                   pl.BlockSpec((tm // tk, ATT_HEAD_DIM, tk), lambda i: (i, 0, 0)),
                   pl.BlockSpec((SUBLANES, tm), lambda i: (0, i))],
        compiler_params=_params("parallel"),
        name="sa_proj",
    )(x2d, g, w, gq, gk, cos, sin, seg, expand)


INT_MIN = -2 ** 31
NEG_INF_KEY = INT_MIN + 0x7FFFFF
ATT_KEY_CHUNK = 256


def _sortable_key(x):
    bits = pltpu.bitcast(jnp.where(x == 0.0, 0.0, x), I32)
    return bits ^ ((bits >> 31) & 0x7FFFFFFF)


def _dsa_body(q_ref, qi_ref, wt_ref, ka_ref, kb_ref, vt_ref, o_ref,
              qe, qo, qie, qio, skey, last_s, m_s, l_s, acc_s, s_s, p_s, *, topk, seq_bits):
    blk = CHUNK
    tk = vt_ref.shape[2]
    j = pl.program_id(1)
    n_chunks = lax.div(j * blk + blk + tk - 1, tk)
    lane = _iota((blk, LANES), 1)
    row = _iota((tk, LANES), 0)
    low = lane < ATT_HEAD_DIM
    q_pos = j * blk + _iota((tk, LANES), 1)

    n_pairs = ATT_HEADS // 2
    for p in range(n_pairs):
        slab = q_ref[0, :, p * LANES:(p + 1) * LANES].astype(F32)
        qe[:, p * blk:(p + 1) * blk] = jnp.where(low, slab, 0.0).T.astype(BF16)
        qo[:, p * blk:(p + 1) * blk] = jnp.where(low, 0.0, slab).T.astype(BF16)
    n_ipairs = IDX_HEADS // 2
    for p in range(n_ipairs):
        slab = qi_ref[0, :, p * LANES:(p + 1) * LANES].astype(F32)
        qie[:, p * blk:(p + 1) * blk] = jnp.where(low, slab, 0.0).T.astype(BF16)
        qio[:, p * blk:(p + 1) * blk] = jnp.where(low, 0.0, slab).T.astype(BF16)

    idx_scale = (IDX_HEADS ** -0.5) * (IDX_DIM ** -0.5)
    wt = wt_ref[...]

    def chunk(ci):
        return pl.ds(pl.multiple_of(ci * tk, tk), tk)

    cw = 2 * LANES

    def score_chunk(ci, _):
        sc = jnp.zeros((tk, LANES), F32)
        for par, (src, k_ref) in enumerate(((qie, kb_ref), (qio, ka_ref))):
            kc = k_ref[0, chunk(ci), :]
            for c0 in range(0, n_ipairs * blk, cw):
                logits = _dot(kc, src[:, c0:c0 + cw])
                for u in range(cw // blk):
                    head = 2 * (c0 // blk + u) + par
                    sc = sc + jnp.maximum(logits[:, u * blk:(u + 1) * blk], 0.0) * wt[head:head + 1, :]
        sc = sc * idx_scale
        sc = jnp.where(ci * tk + row <= q_pos, sc, -jnp.inf)
        skey[chunk(ci), :] = _sortable_key(sc)
        return 0

    lax.fori_loop(0, n_chunks, score_chunk, 0)

    acc_rows = 4 * SUBLANES

    def count(pred_fn):
        def body(ci, c):
            hit = jnp.where(pred_fn(skey[chunk(ci), :], ci), 1.0, 0.0)
            return c + jnp.sum(hit.reshape(tk // acc_rows, acc_rows, LANES), axis=0)
        part = lax.fori_loop(0, n_chunks, body, jnp.zeros((acc_rows, LANES), F32))
        return jnp.sum(part, axis=0, keepdims=True)

    kf = float(topk)
    thr = jnp.full((1, LANES), INT_MIN, I32)
    n_ge = jnp.full((1, LANES), 1.0, F32) * (n_chunks * tk).astype(F32)
    for bit in range(31, -1, -1):
        cand = thr + jnp.int32(INT_MIN if bit == 31 else 1 << bit)
        c = count(lambda key, ci, cand=cand: key >= cand)
        thr = jnp.where(c >= kf, cand, thr)
        n_ge = jnp.where(c >= kf, c, n_ge)

    last_s[...] = jnp.full(last_s.shape, 2 ** 31 - 1, I32)
    tied = (n_ge > kf) & (thr > NEG_INF_KEY)

    @pl.when(jnp.sum(jnp.where(tied, 1.0, 0.0)) > 0.0)
    def _():
        need = kf - count(lambda key, ci: key > thr)
        last = jnp.zeros((1, LANES), I32)
        for bit in range(seq_bits - 1, -1, -1):
            cand = last + jnp.int32(1 << bit)
            c = count(lambda key, ci, cand=cand: (key == thr) & (ci * tk + row < cand))
            last = jnp.where(c <= need - 1.0, cand, last)
        last_s[...] = jnp.broadcast_to(last, last_s.shape)

    last = last_s[0:1, :]

    m_s[...] = jnp.full(m_s.shape, -1e30, F32)
    l_s[...] = jnp.zeros(l_s.shape, F32)
    acc_s[...] = jnp.zeros(acc_s.shape, F32)
    half = n_pairs * blk

    def attn_chunk(ci, _):
        key = skey[chunk(ci), :]
        pos = ci * tk + row
        sel = ((key > thr) | ((key == thr) & (pos <= last))) & (pos <= q_pos)
        bias = jnp.where(sel, 0.0, -jnp.inf)
        bias = jnp.concatenate([bias] * (cw // blk), axis=1)
        m_chunk = []
        for src, k_ref, base in ((qe, ka_ref, 0), (qo, kb_ref, half)):
            kc = k_ref[0, chunk(ci), :]
            for c0 in range(0, half, cw):
                s = _dot(kc, src[:, c0:c0 + cw]) + bias
                s_s[:, base + c0:base + c0 + cw] = s
                m_chunk.append(jnp.max(s, axis=0, keepdims=True))
        m_prev = m_s[...]
        m_cur = jnp.maximum(m_prev, jnp.concatenate(m_chunk, axis=1))
        alpha = jnp.exp2(m_prev - m_cur)
        l_chunk = []
        for c0 in range(0, 2 * half, cw):
            p = jnp.exp2(s_s[:, c0:c0 + cw] - m_cur[:, c0:c0 + cw])
            l_chunk.append(jnp.sum(p, axis=0, keepdims=True))
            p_s[:, c0:c0 + cw] = p.astype(BF16)
        l_s[...] = alpha * l_s[...] + jnp.concatenate(l_chunk, axis=1)
        acc_s[...] = alpha * acc_s[...] + _dot(vt_ref[ci], p_s[...])
        m_s[...] = m_cur
        return 0

    lax.fori_loop(0, n_chunks, attn_chunk, 0)

    out = acc_s[...] / l_s[...]
    for p in range(n_pairs):
        pair = jnp.concatenate([out[:, p * blk:(p + 1) * blk], out[:, half + p * blk:half + (p + 1) * blk]], axis=0)
        o_ref[0, :, p * LANES:(p + 1) * LANES] = pair.T.astype(o_ref.dtype)


def _dsa(q, qi, wt, ka, kb, vt, *, topk):
    b, l, nq = q.shape
    nqi = qi.shape[2]
    nb = l // CHUNK
    tk = vt.shape[2]
    rows = (ATT_HEADS // 2) * CHUNK
    irows = (IDX_HEADS // 2) * CHUNK
    qblk = lambda w: pl.BlockSpec((1, CHUNK, w), lambda i, j: (i, j, 0))
    seq = pl.BlockSpec((1, l, LANES), lambda i, j: (i, 0, 0))
    return pl.pallas_call(
        functools.partial(_dsa_body, topk=topk, seq_bits=int(np.log2(l))),
        out_shape=jax.ShapeDtypeStruct((b, l, nq), BF16),
        grid=(b, nb),
        in_specs=[qblk(nq), qblk(nqi), pl.BlockSpec((SUBLANES, CHUNK), lambda i, j: (0, i * nb + j)),
                  seq, seq, pl.BlockSpec((l // tk, ATT_HEAD_DIM, tk), lambda i, j: (i, 0, 0))],
        out_specs=qblk(nq),
        scratch_shapes=[pltpu.VMEM((LANES, rows), BF16), pltpu.VMEM((LANES, rows), BF16),
                        pltpu.VMEM((LANES, irows), BF16), pltpu.VMEM((LANES, irows), BF16),
                        pltpu.VMEM((l, LANES), I32), pltpu.VMEM((SUBLANES, LANES), I32),
                        pltpu.VMEM((1, 2 * rows), F32), pltpu.VMEM((1, 2 * rows), F32),
                        pltpu.VMEM((ATT_HEAD_DIM, 2 * rows), F32),
                        pltpu.VMEM((tk, 2 * rows), F32), pltpu.VMEM((tk, 2 * rows), BF16)],
        compiler_params=_params("parallel", "arbitrary"),
        name="dsa",
    )(q, qi, wt, ka, kb, vt)


def _split_cols(w, sizes):
    out, c0 = [], 0
    for s in sizes:
        out.append(w[:, c0:c0 + s])
        c0 += s
    return out


def _pad_cols(w, width):
    return jnp.pad(w, ((0, 0), (0, width - w.shape[1])))


def _pad_row(v, width):
    return jnp.pad(v, (0, width - v.shape[0]))[None, :]


def _hybrid_mixer_parts(x2d, b, l, mix_norm, w_in, conv_w, conv_b, dt_bias, a_log, d_skip, ssd_norm,
                        igate_bias, fgate_bias, mlstm_norm, *, tm):
    wz, wxbc, wdt, wq, wk, wv, wo, wi, wf = _split_cols(w_in, HY_SPLITS)
    w = jnp.concatenate([wz, wxbc, wq, wk, wv, wo, _pad_cols(wdt, LANES), _pad_cols(wi, LANES),
                         _pad_cols(wf, LANES)], axis=1).astype(BF16)
    z, xc, qm, km, vm, om, dts, ig, fg = _hy_proj(x2d, mix_norm[None, :], w, conv_w, conv_b[None, :], l, tm=tm)
    acts = [a.reshape(b, l, a.shape[1]) for a in (z, xc, dts, qm, km, vm, om, ig, fg)]
    params = [_pad_row(dt_bias, LANES), _pad_row(a_log, LANES), jnp.repeat(d_skip, SSD_HEAD_DIM)[None, :],
              ssd_norm[None, :], _pad_row(igate_bias, LANES), _pad_row(fgate_bias, LANES), mlstm_norm[None, :]]
    y, hm = _mixer(acts, params)
    return y.reshape(b * l, SSD_WIDTH), hm.reshape(b * l, ML_WIDTH)


def _rope_tables(positions):
    half = ATT_HEAD_DIM // 2
    inv = ROPE_THETA ** (-jnp.arange(0, ATT_HEAD_DIM, 2, dtype=F32) / ATT_HEAD_DIM)
    ang = positions.astype(F32)[..., None] * inv
    cos, sin = jnp.cos(ang), jnp.sin(ang)
    reps = LANES // half
    cos_t = jnp.tile(cos, (1, 1, reps))
    sin_t = jnp.tile(jnp.concatenate([-sin, sin], axis=-1), (1, 1, reps // 2))
    t = positions.shape[0] * positions.shape[1]
    return cos_t.reshape(t, LANES), sin_t.reshape(t, LANES)


def _sparse_attention_parts(x2d, b, l, cos, sin, mix_norm, w_in, q_norm, k_norm, topk, *, tm):
    wq, wk, wv, wqi, wki, wwi = _split_cols(w_in, SA_SPLITS)
    w = jnp.concatenate([wq, wqi, wk, wki, _pad_cols(wv, LANES), _pad_cols(wwi, LANES)], axis=1).astype(BF16)
    gq = jnp.tile(q_norm, ATT_HEADS)[None, :]
    gk = _pad_row(k_norm, LANES)
    q, qi, ka, kb, vt, wt = _sa_proj(x2d, mix_norm[None, :], w, gq, gk, cos, sin, tm=tm, tk=ATT_KEY_CHUNK)
    r3 = lambda a: a.reshape(b, l, a.shape[1])
    o = _dsa(r3(q), r3(qi), wt, r3(ka), r3(kb), vt, topk=topk)
    return o.reshape(b * l, ATT_HEADS * ATT_HEAD_DIM)


def kernel(x, positions, ffn_norm, ffn_w_gate, ffn_w_up, ffn_w_down, mix_norm, hy_w_in, hy_conv_w, hy_conv_b,
           hy_dt_bias, hy_a_log, hy_d_skip, hy_ssd_norm, hy_igate_bias, hy_fgate_bias, hy_mlstm_norm, hy_w_out,
           sa_w_in, sa_q_norm, sa_k_norm, sa_w_out):
    b, l, d = x.shape
    depth = ffn_norm.shape[0]
    t = b * l
    tm = 512 if t % 512 == 0 else CHUNK
    topk = min(TOPK_MAX, l // 4)
    cos, sin = _rope_tables(positions)
    wg, wu, wd = ffn_w_gate.astype(BF16), ffn_w_up.astype(BF16), ffn_w_down.astype(BF16)
    x2d = x.reshape(t, d)
    for layer in range(depth):
        x2d = _ffn(x2d, ffn_norm[layer, 0][None, :], wg, wu, wd, layer, 0, tm=tm)
        if layer % 2 == 0:
            e = layer // 2
            y, hm = _hybrid_mixer_parts(x2d, b, l, mix_norm[layer], hy_w_in[e], hy_conv_w[e], hy_conv_b[e],
                                        hy_dt_bias[e], hy_a_log[e], hy_d_skip[e], hy_ssd_norm[e],
                                        hy_igate_bias[e], hy_fgate_bias[e], hy_mlstm_norm[e], tm=tm)
            w_out = hy_w_out[e].astype(BF16)
            acts, ws = [y, hm], [w_out[:SSD_WIDTH], w_out[SSD_WIDTH:]]
        else:
            o = layer // 2
            att = _sparse_attention_parts(x2d, b, l, cos, sin, mix_norm[layer], sa_w_in[o], sa_q_norm[o],
                                          sa_k_norm[o], topk, tm=tm)
            acts, ws = [att], [sa_w_out[o].astype(BF16)]
        x2d = _proj_ffn(x2d, acts, ws, ffn_norm[layer, 1][None, :], wg, wu, wd, layer, 1, tm=tm)
    return x2d.reshape(b, l, d)
```

```python
import functools

import jax
import jax.numpy as jnp
import numpy as np
from jax import lax
from jax.experimental import pallas as pl
from jax.experimental.pallas import tpu as pltpu

F32 = jnp.float32
BF16 = jnp.bfloat16
I32 = jnp.int32

EPS = 1e-6
ROPE_THETA = 10000.0
LOG2_E = 1.4426950408889634
LANES = 128
SUBLANES = 8
CHUNK = 128
CONV_K = 4
TOPK_MAX = 256

D_MODEL = 1024
D_FF = 2816
SSD_WIDTH = 1024
SSD_HEAD_DIM = 64
SSD_HEADS = 16
SSD_GROUPS = 2
SSD_STATE = 128
SSD_CONV_DIM = SSD_WIDTH + 2 * SSD_GROUPS * SSD_STATE
ML_WIDTH = 1024
ML_HEADS = 4
ML_V_DIM = 256
ML_QK_DIM = 128
HY_SPLITS = (SSD_WIDTH, SSD_CONV_DIM, SSD_HEADS, ML_HEADS * ML_QK_DIM, ML_HEADS * ML_QK_DIM,
             ML_WIDTH, ML_WIDTH, ML_HEADS, ML_HEADS)
ATT_HEAD_DIM = 64
ATT_HEADS = 16
IDX_HEADS = 8
IDX_DIM = 64
SA_SPLITS = (ATT_HEADS * ATT_HEAD_DIM, ATT_HEAD_DIM, ATT_HEAD_DIM, IDX_HEADS * IDX_DIM, IDX_DIM, IDX_HEADS)

VMEM_LIMIT_BYTES = 56 * 1024 * 1024

NT_DIMS = (((1,), (1,)), ((), ()))


def _params(*sem):
    return pltpu.CompilerParams(dimension_semantics=sem, vmem_limit_bytes=VMEM_LIMIT_BYTES)


def _resident(shape):
    nd = len(shape)
    return pl.BlockSpec(shape, lambda *_: (0,) * nd, pipeline_mode=pl.Buffered(1))


def _dot(a, b):
    return jnp.dot(a, b, preferred_element_type=F32)


def _dot_nt(a, b):
    return lax.dot_general(a, b, NT_DIMS, preferred_element_type=F32)


def _split3(a):
    a1 = a.astype(BF16)
    r = a - a1.astype(F32)
    a2 = r.astype(BF16)
    r = r - a2.astype(F32)
    return a1, a2, r.astype(BF16)


N_SPLIT = 3


def _dot_f32_lhs(a, b01_stack):
    return _dot(jnp.concatenate(_split3(a), axis=1), b01_stack)


def _dot_f32_rhs(a01_stack, b):
    return _dot(a01_stack, jnp.concatenate(_split3(b), axis=0))


def _rms(x, g):
    return x * lax.rsqrt(jnp.mean(x * x, axis=-1, keepdims=True) + EPS) * g


def _sigmoid(x):
    return 1.0 / (1.0 + jnp.exp(-x))


def _softplus(x):
    return jnp.maximum(x, 0.0) + jnp.log1p(jnp.exp(-jnp.abs(x)))


def _iota(shape, dim):
    return lax.broadcasted_iota(I32, shape, dim)


def _expander(rows, cols, width):
    return np.arange(cols)[None, :] // width == np.arange(rows)[:, None]


def _lhs_stack(b01):
    return jnp.asarray(np.tile(b01, (N_SPLIT, 1)), dtype=BF16)


def _rhs_stack(a01):
    return jnp.asarray(np.tile(a01, (1, N_SPLIT)), dtype=BF16)


def _stacked(w, layer, idx):
    return pl.BlockSpec((None, None) + w.shape[2:], lambda *_: (layer, idx, 0, 0), pipeline_mode=pl.Buffered(1))


def _tril(n):
    return (_iota((n, n), 0) >= _iota((n, n), 1))


def _swiglu_half(x, g_ref, wg_ref, wu_ref, wd_ref, ff_chunk):
    h = _rms(x, g_ref[...]).astype(BF16)
    acc = jnp.zeros(x.shape, F32)
    for c0 in range(0, wg_ref.shape[1], ff_chunk):
        gate = _dot(h, wg_ref[:, c0:c0 + ff_chunk])
        up = _dot(h, wu_ref[:, c0:c0 + ff_chunk])
        act = (gate * _sigmoid(gate) * up).astype(BF16)
        acc = acc + _dot(act, wd_ref[c0:c0 + ff_chunk, :])
    return x + 0.5 * acc


def _ffn_body(x_ref, g_ref, wg_ref, wu_ref, wd_ref, o_ref, *, ff_chunk):
    o_ref[...] = _swiglu_half(x_ref[...], g_ref, wg_ref, wu_ref, wd_ref, ff_chunk)


def _ffn(x2d, g, wg, wu, wd, layer, idx, *, tm, ff_chunk=256):
    t, d = x2d.shape
    return pl.pallas_call(
        functools.partial(_ffn_body, ff_chunk=ff_chunk),
        out_shape=jax.ShapeDtypeStruct((t, d), F32),
        grid=(t // tm,),
        in_specs=[pl.BlockSpec((tm, d), lambda i: (i, 0)), _resident(g.shape),
                  _stacked(wg, layer, idx), _stacked(wu, layer, idx), _stacked(wd, layer, idx)],
        out_specs=pl.BlockSpec((tm, d), lambda i: (i, 0)),
        compiler_params=_params("parallel"),
        name="ffn",
    )(x2d, g, wg, wu, wd)


def _proj_ffn_body(*refs, n_in, ff_chunk):
    x_ref = refs[0]
    a_refs = refs[1:1 + n_in]
    w_refs = refs[1 + n_in:1 + 2 * n_in]
    g_ref, wg_ref, wu_ref, wd_ref, o_ref = refs[1 + 2 * n_in:]
    x = x_ref[...]
    for a_ref, w_ref in zip(a_refs, w_refs):
        x = x + _dot(a_ref[...], w_ref[...])
    o_ref[...] = _swiglu_half(x, g_ref, wg_ref, wu_ref, wd_ref, ff_chunk)


def _proj_ffn(x2d, acts, ws, g, wg, wu, wd, layer, idx, *, tm, ff_chunk=256):
    t, d = x2d.shape
    n_in = len(acts)
    return pl.pallas_call(
        functools.partial(_proj_ffn_body, n_in=n_in, ff_chunk=ff_chunk),
        out_shape=jax.ShapeDtypeStruct((t, d), F32),
        grid=(t // tm,),
        in_specs=([pl.BlockSpec((tm, d), lambda i: (i, 0))]
                  + [pl.BlockSpec((tm, a.shape[1]), lambda i: (i, 0)) for a in acts]
                  + [_resident(w.shape) for w in ws]
                  + [_resident(g.shape), _stacked(wg, layer, idx), _stacked(wu, layer, idx),
                     _stacked(wd, layer, idx)]),
        out_specs=pl.BlockSpec((tm, d), lambda i: (i, 0)),
        compiler_params=_params("parallel"),
        name="proj_ffn",
    )(x2d, *acts, *ws, g, wg, wu, wd)


def _hy_proj_body(x_ref, g_ref, w_ref, cw_ref, cb_ref, dtb_ref, ib_ref, fb_ref, z_ref, xc_ref, q_ref, k_ref, v_ref,
                  o_ref, dt_ref, i_ref, f_ref, tails, *, tiles_per_seq):
    tm = x_ref.shape[0]
    lane = _iota((tm, LANES), 1)
    ssd_head, ml_head = lane < SSD_HEADS, lane < ML_HEADS
    same = lambda a: a
    outputs = (
        (z_ref, lambda a: a * _sigmoid(a)),
        (None, None),
        (q_ref, lambda a: a * (ML_QK_DIM ** -0.5)),
        (k_ref, same),
        (v_ref, same),
        (o_ref, _sigmoid),
        (dt_ref, lambda a: jnp.where(ssd_head, _softplus(a + dtb_ref[...]), 0.0)),
        (i_ref, lambda a: jnp.where(ml_head, a + ib_ref[...], 0.0)),
        (f_ref, lambda a: jnp.where(ml_head, -_softplus(-(a + fb_ref[...])), 0.0)),
    )

    @pl.when(pl.program_id(0) % tiles_per_seq == 0)
    def _():
        tails[...] = jnp.zeros(tails.shape, F32)

    h = _rms(x_ref[...], g_ref[...]).astype(BF16)

    u = _dot(h, w_ref[:, SSD_WIDTH:SSD_WIDTH + SSD_CONV_DIM])
    first_row = _iota((SUBLANES, SSD_CONV_DIM), 0) == 0
    part = cw_ref[0:1, :] * u
    for k in range(1, CONV_K):
        rolled = pltpu.roll(part, 1, 0)
        head = jnp.where(first_row, pltpu.roll(tails[k - 1], 1, 0), rolled[:SUBLANES])
        tails[k - 1] = part[tm - SUBLANES:, :]
        part = jnp.concatenate([head, rolled[SUBLANES:]], axis=0) + cw_ref[k:k + 1, :] * u
    conv = part + cb_ref[...]
    xc_ref[...] = (conv * _sigmoid(conv)).astype(xc_ref.dtype)

    c0 = 0
    for out_ref, finish in outputs:
        if out_ref is None:
            c0 += SSD_CONV_DIM
            continue
        wd = out_ref.shape[1]
        out_ref[...] = finish(_dot(h, w_ref[:, c0:c0 + wd])).astype(out_ref.dtype)
        c0 += wd


def _hy_proj(x2d, g, w, conv_w, conv_b, dt_bias, ib, fb, seq_len, *, tm):
    t, d = x2d.shape
    kw = ML_HEADS * ML_QK_DIM
    widths = (SSD_WIDTH, SSD_CONV_DIM, kw, kw, ML_WIDTH, ML_WIDTH, LANES, LANES, LANES)
    return pl.pallas_call(
        functools.partial(_hy_proj_body, tiles_per_seq=seq_len // tm),
        out_shape=[jax.ShapeDtypeStruct((t, wd), F32) for wd in widths],
        grid=(t // tm,),
        in_specs=[pl.BlockSpec((tm, d), lambda i: (i, 0)), _resident(g.shape), _resident(w.shape),
                  _resident(conv_w.shape), _resident(conv_b.shape), _resident(dt_bias.shape),
                  _resident(ib.shape), _resident(fb.shape)],
        out_specs=[pl.BlockSpec((tm, wd), lambda i: (i, 0)) for wd in widths],
        scratch_shapes=[pltpu.VMEM((CONV_K - 1, SUBLANES, SSD_CONV_DIM), F32)],
        compiler_params=_params("arbitrary"),
        name="hy_proj",
    )(x2d, g, w, conv_w, conv_b, dt_bias, ib, fb)


def _ssd_body(zg_ref, xc_ref, dt_ref, alog_ref, dskip_ref, nrm_ref, tril_ref, expand_ref, y_ref, state):
    q_len = CHUNK
    hp = SSD_WIDTH // SSD_GROUPS
    xc = xc_ref[0]
    xs = xc[:, :SSD_WIDTH]
    bm = xc[:, SSD_WIDTH:SSD_WIDTH + SSD_GROUPS * SSD_STATE]
    cm = xc[:, SSD_WIDTH + SSD_GROUPS * SSD_STATE:]

    dt = dt_ref[0]
    a = dt * (-jnp.exp(alog_ref[...]))
    tril = _tril(q_len)
    acs = _dot_f32_rhs(tril_ref[...], a)
    eacs = jnp.exp(acs)
    dend = jnp.exp(acs[q_len - 1:q_len, :] - acs)
    wide = _dot_f32_lhs(jnp.concatenate([dt, eacs, dend], axis=0), expand_ref[...])
    dt_x, eacs_x, dend_x = wide[:q_len], wide[q_len:2 * q_len], wide[2 * q_len:]

    xd = xs * dt_x
    lane_w = _iota((q_len, SSD_WIDTH), 1)
    low_half = (lane_w % LANES) < SSD_HEAD_DIM
    xd_halves = (jnp.where(low_half, xd, 0.0).astype(BF16), jnp.where(low_half, 0.0, xd).astype(BF16))
    xdd_b = (xd * dend_x).astype(BF16)
    acs_t = acs.T
    st = state[...]
    st_b = st.astype(BF16)

    y_diag, y_off, s_new = [], [], []
    heads_per_group = SSD_HEADS // SSD_GROUPS
    for g in range(SSD_GROUPS):
        bg = bm[:, g * SSD_STATE:(g + 1) * SSD_STATE]
        cg_b = cm[:, g * SSD_STATE:(g + 1) * SSD_STATE].astype(BF16)
        cb = _dot_nt(cg_b, bg.astype(BF16))
        s_new.append(_dot(bg.T.astype(BF16), xdd_b[:, g * hp:(g + 1) * hp]))
        y_off.append(_dot(cg_b, st_b[:, g * hp:(g + 1) * hp]))
        for pair in range(heads_per_group // 2):
            acc = None
            for par in range(2):
                h = g * heads_per_group + 2 * pair + par
                seg = acs[:, h:h + 1] - acs_t[h:h + 1, :]
                m = (cb * jnp.exp(jnp.where(tril, seg, -jnp.inf))).astype(BF16)
                slab = (h // 2) * LANES
                d = _dot(m, xd_halves[par][:, slab:slab + LANES])
                acc = d if acc is None else acc + d
            y_diag.append(acc)
    y = (jnp.concatenate(y_diag, axis=1) + eacs_x * jnp.concatenate(y_off, axis=1) + xs * dskip_ref[...])
    state[...] = st * eacs_x[q_len - 1:q_len, :] + jnp.concatenate(s_new, axis=1)

    yg = y * zg_ref[0]
    nrm = nrm_ref[...]
    y_ref[0] = jnp.concatenate(
        [_rms(yg[:, g * hp:(g + 1) * hp], nrm[:, g * hp:(g + 1) * hp]) for g in range(SSD_GROUPS)],
        axis=1).astype(y_ref.dtype)


def _mlstm_body(q_ref, k_ref, v_ref, og_ref, i_ref, f_ref, nrm_ref, tril_ref, exp_v_ref, exp_k_ref,
                h_ref, c_st, n_st, m_st):
    q_len = CHUNK

    li = i_ref[0]
    lf = f_ref[0]
    tril = _tril(q_len)
    bcs = _dot_f32_rhs(tril_ref[...], lf)
    b_last = bcs[q_len - 1:q_len, :]
    gl = b_last - bcs + li
    m_loc = jnp.max(gl, axis=0, keepdims=True)
    wg = jnp.exp(gl - m_loc)
    m_in = m_st[0:1, :]
    m_new = jnp.maximum(b_last + m_in, m_loc)
    a_sc = jnp.exp(b_last + m_in - m_new)
    g_sc = jnp.exp(m_loc - m_new)

    sc = jnp.concatenate([wg, jnp.broadcast_to(a_sc, (SUBLANES, LANES)),
                          jnp.broadcast_to(g_sc, (SUBLANES, LANES))], axis=0)
    sc_v = _dot_f32_lhs(sc, exp_v_ref[...])
    sc_k = _dot_f32_lhs(sc, exp_k_ref[...])
    wg_v, a_v, g_v = sc_v[:q_len], sc_v[q_len:q_len + 1], sc_v[q_len + SUBLANES:q_len + SUBLANES + 1]
    wg_k, a_k, g_k = sc_k[:q_len], sc_k[q_len:q_len + 1], sc_k[q_len + SUBLANES:q_len + SUBLANES + 1]

    q = q_ref[0]
    k = k_ref[0]
    v = v_ref[0]
    q_b, k_b, v_b = q.astype(BF16), k.astype(BF16), v.astype(BF16)
    vw_b = (v * wg_v).astype(BF16)
    n_in = n_st[0:1, :]
    n_loc = jnp.sum(k * wg_k, axis=0, keepdims=True)
    d_t = (li - bcs).T

    outs, c_new = [], []
    for h in range(ML_HEADS):
        ks = slice(h * ML_QK_DIM, (h + 1) * ML_QK_DIM)
        vs = slice(h * ML_V_DIM, (h + 1) * ML_V_DIM)
        c_new.append(_dot(k[:, ks].T.astype(BF16), vw_b[:, vs]))
        dm = jnp.where(tril, bcs[:, h:h + 1] + d_t[h:h + 1, :], -jnp.inf)
        inter = bcs[:, h:h + 1] + m_in[:, h:h + 1]
        m_t = jnp.maximum(jnp.max(dm, axis=1, keepdims=True), inter)
        s = _dot_nt(q_b[:, ks], k_b[:, ks]) * jnp.exp(dm - m_t)
        w_int = jnp.exp(inter - m_t)
        num = _dot(s.astype(BF16), v_b[:, vs]) + w_int * _dot(q_b[:, ks], c_st[:, vs].astype(BF16))
        den = (jnp.sum(s, axis=1, keepdims=True)
               + w_int * jnp.sum(q[:, ks] * n_in[:, ks], axis=1, keepdims=True))
        outs.append(num / jnp.maximum(jnp.abs(den), jnp.exp(-m_t)))
    nrm = nrm_ref[...]
    hn = jnp.concatenate([_rms(outs[h], nrm[:, h * ML_V_DIM:(h + 1) * ML_V_DIM]) for h in range(ML_HEADS)],
                         axis=1)
    h_ref[0] = (hn * og_ref[0]).astype(h_ref.dtype)

    c_st[...] = a_v * c_st[...] + g_v * jnp.concatenate(c_new, axis=1)
    n_st[...] = jnp.broadcast_to(a_k * n_in + g_k * n_loc, n_st.shape)
    m_st[...] = jnp.broadcast_to(m_new, m_st.shape)


MIXER_SEQS_PER_STEP = 4


def _mixer_body(z_ref, xc_ref, dt_ref, q_ref, k_ref, v_ref, o_ref, i_ref, f_ref,
                alog_ref, dskip_ref, snrm_ref, mnrm_ref,
                tril_ref, exp_h_ref, exp_v_ref, exp_k_ref, y_ref, h_ref, state, c_st, n_st, m_st):
    @pl.when(pl.program_id(1) == 0)
    def _():
        for st_ref in (state, c_st, n_st, m_st):
            st_ref[...] = jnp.zeros(st_ref.shape, F32)

    for bi in range(z_ref.shape[0]):
        one = lambda ref: ref.at[bi:bi + 1]
        _ssd_body(one(z_ref), one(xc_ref), one(dt_ref), alog_ref, dskip_ref, snrm_ref, tril_ref, exp_h_ref,
                  one(y_ref), state.at[bi])
        _mlstm_body(one(q_ref), one(k_ref), one(v_ref), one(o_ref), one(i_ref), one(f_ref), mnrm_ref,
                    tril_ref, exp_v_ref, exp_k_ref, one(h_ref), c_st.at[bi], n_st.at[bi], m_st.at[bi])


def _mixer(acts, params):
    b, l, _ = acts[0].shape
    kw = ML_HEADS * ML_QK_DIM
    consts = [_rhs_stack(np.tril(np.ones((CHUNK, CHUNK)))),
              _lhs_stack(_expander(LANES, SSD_WIDTH, SSD_HEAD_DIM)), _lhs_stack(_expander(LANES, ML_WIDTH, ML_V_DIM)),
              _lhs_stack(_expander(LANES, kw, ML_QK_DIM))]
    nb = MIXER_SEQS_PER_STEP if b % MIXER_SEQS_PER_STEP == 0 else 1
    blk = lambda w: pl.BlockSpec((nb, CHUNK, w), lambda i, c: (i, c, 0))
    return pl.pallas_call(
        _mixer_body,
        out_shape=[jax.ShapeDtypeStruct((b, l, SSD_WIDTH), BF16), jax.ShapeDtypeStruct((b, l, ML_WIDTH), BF16)],
        grid=(b // nb, l // CHUNK),
        in_specs=([blk(a.shape[2]) for a in acts] + [_resident(p.shape) for p in params]
                  + [_resident(c.shape) for c in consts]),
        out_specs=[blk(SSD_WIDTH), blk(ML_WIDTH)],
        scratch_shapes=[pltpu.VMEM((nb, SSD_STATE, SSD_WIDTH), F32),
                        pltpu.VMEM((nb, ML_QK_DIM, ML_WIDTH), F32),
                        pltpu.VMEM((nb, SUBLANES, kw), F32),
                        pltpu.VMEM((nb, SUBLANES, LANES), F32)],
        compiler_params=_params("parallel", "arbitrary"),
        name="mixer",
    )(*acts, *params, *consts)


def _rope(x, cos, sin, first_half):
    partner = jnp.where(first_half, pltpu.roll(x, LANES - ATT_HEAD_DIM // 2, 1), pltpu.roll(x, ATT_HEAD_DIM // 2, 1))
    return x * cos + partner * sin


def _sa_proj_body(x_ref, g_ref, w_ref, gq_ref, gk_ref, cos_ref, sin_ref, seg_ref, expand_ref,
                  qe_ref, qo_ref, qie_ref, qio_ref, ka_ref, kb_ref, vt_ref, wt_ref):
    tm = x_ref.shape[0]
    tk = vt_ref.shape[2]
    nq = ATT_HEADS * ATT_HEAD_DIM
    nqi = IDX_HEADS * IDX_DIM
    h = _rms(x_ref[...], g_ref[...]).astype(BF16)
    proj = _dot(h, w_ref[...])
    cos, sin = cos_ref[...], sin_ref[...]
    lane = _iota((tm, LANES), 1)
    first_half = (lane % ATT_HEAD_DIM) < ATT_HEAD_DIM // 2
    low_head = lane < ATT_HEAD_DIM

    q = proj[:, :nq]
    ms = _dot_f32_lhs(q * q, seg_ref[...]) * (1.0 / ATT_HEAD_DIM)
    rs = _dot_f32_lhs(lax.rsqrt(ms + EPS), expand_ref[...])
    qn = q * rs * gq_ref[...]
    q_scale = ATT_HEAD_DIM ** -0.5 * LOG2_E

    def emit(even_ref, odd_ref, slab, s0):
        low = _iota((CHUNK, LANES), 1) < ATT_HEAD_DIM
        for b0 in range(0, tm, CHUNK):
            rows = slab[b0:b0 + CHUNK]
            even_ref[b0 // CHUNK, :, s0:s0 + LANES] = jnp.where(low, rows, 0.0).T.astype(BF16)
            odd_ref[b0 // CHUNK, :, s0:s0 + LANES] = jnp.where(low, 0.0, rows).T.astype(BF16)

    for s0 in range(0, nq, LANES):
        emit(qe_ref, qo_ref, _rope(qn[:, s0:s0 + LANES], cos, sin, first_half) * q_scale, s0)
    for s0 in range(0, nqi, LANES):
        emit(qie_ref, qio_ref, _rope(proj[:, nq + s0:nq + s0 + LANES], cos, sin, first_half), s0)

    kk = proj[:, nq + nqi:nq + nqi + LANES]
    k_ms = jnp.sum(jnp.where(low_head, kk * kk, 0.0), axis=1, keepdims=True) * (1.0 / ATT_HEAD_DIM)
    kk = jnp.where(low_head, kk * lax.rsqrt(k_ms + EPS) * gk_ref[...], kk)
    kk = _rope(kk, cos, sin, first_half)
    ka_ref[...] = kk.astype(BF16)
    kb_ref[...] = pltpu.roll(kk, ATT_HEAD_DIM, 1).astype(BF16)
    vv = proj[:, nq + nqi + LANES:nq + nqi + 2 * LANES]
    for c in range(tm // tk):
        vt_ref[c] = vv[c * tk:(c + 1) * tk, :].T[:ATT_HEAD_DIM, :].astype(BF16)
    wt_ref[...] = proj[:, nq + nqi + 2 * LANES:].T[:SUBLANES, :]


def _sa_proj(x2d, g, w, gq, gk, cos, sin, *, tm, tk):
    t, d = x2d.shape
    nq = ATT_HEADS * ATT_HEAD_DIM
    nqi = IDX_HEADS * IDX_DIM
    row = lambda wd: pl.BlockSpec((tm, wd), lambda i: (i, 0))
    qblocks = lambda wd: jax.ShapeDtypeStruct((t // CHUNK, LANES, wd), BF16)
    qspec = lambda wd: pl.BlockSpec((tm // CHUNK, LANES, wd), lambda i: (i, 0, 0))
    outs = [qblocks(nq), qblocks(nq), qblocks(nqi), qblocks(nqi),
            jax.ShapeDtypeStruct((t, LANES), BF16), jax.ShapeDtypeStruct((t, LANES), BF16),
            jax.ShapeDtypeStruct((t // tk, ATT_HEAD_DIM, tk), BF16),
            jax.ShapeDtypeStruct((SUBLANES, t), F32)]
    head_of_lane = _expander(LANES, nq, ATT_HEAD_DIM)
    expand, seg = _lhs_stack(head_of_lane), _lhs_stack(head_of_lane.T)
    return pl.pallas_call(
        _sa_proj_body,
        out_shape=outs,
        grid=(t // tm,),
        in_specs=[row(d), _resident(g.shape), _resident(w.shape), _resident(gq.shape), _resident(gk.shape),
                  row(LANES), row(LANES), _resident(seg.shape), _resident(expand.shape)],
        out_specs=[qspec(nq), qspec(nq), qspec(nqi), qspec(nqi), row(LANES), row(LANES),
                   pl.BlockSpec((tm // tk, ATT_HEAD_DIM, tk), lambda i: (i, 0, 0)),
                   pl.BlockSpec((SUBLANES, tm), lambda i: (0, i))],
        compiler_params=_params("parallel"),
        name="sa_proj",
    )(x2d, g, w, gq, gk, cos, sin, seg, expand)


INT_MIN = -2 ** 31
NEG_INF_KEY = INT_MIN + 0x7FFFFF
ATT_KEY_CHUNK = 256


def _sortable_key(x):
    bits = pltpu.bitcast(jnp.where(x == 0.0, 0.0, x), I32)
    return bits ^ ((bits >> 31) & 0x7FFFFFFF)


def _dsa_body(qe_ref, qo_ref, qie_ref, qio_ref, wt_ref, ka_ref, kb_ref, vt_ref, o_ref,
              skey, last_s, m_s, l_s, acc_s, s_s, p_s, *, topk, seq_bits):
    blk = CHUNK
    tk = vt_ref.shape[2]
    j = pl.program_id(1)
    n_chunks = lax.div(j * blk + blk + tk - 1, tk)
    row = _iota((tk, LANES), 0)
    q_pos = j * blk + _iota((tk, LANES), 1)

    qe, qo, qie, qio = qe_ref.at[0], qo_ref.at[0], qie_ref.at[0], qio_ref.at[0]
    n_pairs = ATT_HEADS // 2
    n_ipairs = IDX_HEADS // 2

    idx_scale = (IDX_HEADS ** -0.5) * (IDX_DIM ** -0.5)
    wt = wt_ref[...]

    def chunk(ci):
        return pl.ds(pl.multiple_of(ci * tk, tk), tk)

    cw = 2 * LANES

    def score_chunk(ci, _):
        sc = jnp.zeros((tk, LANES), F32)
        for par, (src, k_ref) in enumerate(((qie, kb_ref), (qio, ka_ref))):
            kc = k_ref[0, chunk(ci), :]
            for c0 in range(0, n_ipairs * blk, cw):
                logits = _dot(kc, src[:, c0:c0 + cw])
                for u in range(cw // blk):
                    head = 2 * (c0 // blk + u) + par
                    sc = sc + jnp.maximum(logits[:, u * blk:(u + 1) * blk], 0.0) * wt[head:head + 1, :]
        sc = sc * idx_scale
        sc = jnp.where(ci * tk + row <= q_pos, sc, -jnp.inf)
        skey[chunk(ci), :] = _sortable_key(sc)
        return 0

    lax.fori_loop(0, n_chunks, score_chunk, 0)

    acc_rows = 4 * SUBLANES

    def count(pred_fn):
        def body(ci, c):
            hit = jnp.where(pred_fn(skey[chunk(ci), :], ci), 1.0, 0.0)
            return c + jnp.sum(hit.reshape(tk // acc_rows, acc_rows, LANES), axis=0)
        part = lax.fori_loop(0, n_chunks, body, jnp.zeros((acc_rows, LANES), F32))
        return jnp.sum(part, axis=0, keepdims=True)

    kf = float(topk)
    thr = jnp.full((1, LANES), INT_MIN, I32)
    n_ge = jnp.full((1, LANES), 1.0, F32) * (n_chunks * tk).astype(F32)
    for bit in range(31, -1, -1):
        cand = thr + jnp.int32(INT_MIN if bit == 31 else 1 << bit)
        c = count(lambda key, ci, cand=cand: key >= cand)
        thr = jnp.where(c >= kf, cand, thr)
        n_ge = jnp.where(c >= kf, c, n_ge)

    last_s[...] = jnp.full(last_s.shape, 2 ** 31 - 1, I32)
    tied = (n_ge > kf) & (thr > NEG_INF_KEY)

    @pl.when(jnp.sum(jnp.where(tied, 1.0, 0.0)) > 0.0)
    def _():
        need = kf - count(lambda key, ci: key > thr)
        last = jnp.zeros((1, LANES), I32)
        for bit in range(seq_bits - 1, -1, -1):
            cand = last + jnp.int32(1 << bit)
            c = count(lambda key, ci, cand=cand: (key == thr) & (ci * tk + row < cand))
            last = jnp.where(c <= need - 1.0, cand, last)
        last_s[...] = jnp.broadcast_to(last, last_s.shape)

    last = last_s[0:1, :]

    m_s[...] = jnp.full(m_s.shape, -1e30, F32)
    l_s[...] = jnp.zeros(l_s.shape, F32)
    acc_s[...] = jnp.zeros(acc_s.shape, F32)
    half = n_pairs * blk

    def masked_logits(ci):
        key = skey[chunk(ci), :]
        pos = ci * tk + row
        sel = ((key > thr) | ((key == thr) & (pos <= last))) & (pos <= q_pos)
        bias = jnp.where(sel, 0.0, -jnp.inf)
        bias = jnp.concatenate([bias] * (cw // blk), axis=1)
        for src, k_ref, base in ((qe, ka_ref, 0), (qo, kb_ref, half)):
            kc = k_ref[0, chunk(ci), :]
            for c0 in range(0, half, cw):
                yield base + c0, _dot(kc, src[:, c0:c0 + cw]) + bias

    def attn_chunk(ci, _):
        m_chunk = []
        for c0, s in masked_logits(ci):
            s_s[:, c0:c0 + cw] = s
            m_chunk.append(jnp.max(s, axis=0, keepdims=True))
        m_prev = m_s[...]
        m_cur = jnp.maximum(m_prev, jnp.concatenate(m_chunk, axis=1))
        alpha = jnp.exp2(m_prev - m_cur)
        l_chunk = []
        for c0 in range(0, 2 * half, cw):
            p = jnp.exp2(s_s[:, c0:c0 + cw] - m_cur[:, c0:c0 + cw])
            l_chunk.append(jnp.sum(p, axis=0, keepdims=True))
            p_s[:, c0:c0 + cw] = p.astype(BF16)
        l_s[...] = alpha * l_s[...] + jnp.concatenate(l_chunk, axis=1)
        acc_s[...] = alpha * acc_s[...] + _dot(vt_ref[ci], p_s[...])
        m_s[...] = m_cur
        return 0

    lax.fori_loop(0, n_chunks, attn_chunk, 0)

    out = acc_s[...] / l_s[...]
    for p in range(n_pairs):
        pair = jnp.concatenate([out[:, p * blk:(p + 1) * blk], out[:, half + p * blk:half + (p + 1) * blk]], axis=0)
        o_ref[0, :, p * LANES:(p + 1) * LANES] = pair.T.astype(o_ref.dtype)


def _dsa(qe, qo, qie, qio, wt, ka, kb, vt, *, topk):
    b, l, _ = ka.shape
    nb = l // CHUNK
    tk = vt.shape[2]
    rows = (ATT_HEADS // 2) * CHUNK
    nq = ATT_HEADS * ATT_HEAD_DIM
    qop = lambda a: pl.BlockSpec((1,) + a.shape[1:], lambda i, j: (i * nb + j, 0, 0))
    seq = pl.BlockSpec((1, l, LANES), lambda i, j: (i, 0, 0))
    return pl.pallas_call(
        functools.partial(_dsa_body, topk=topk, seq_bits=int(np.log2(l))),
        out_shape=jax.ShapeDtypeStruct((b, l, nq), BF16),
        grid=(b, nb),
        in_specs=[qop(qe), qop(qo), qop(qie), qop(qio),
                  pl.BlockSpec((SUBLANES, CHUNK), lambda i, j: (0, i * nb + j)),
                  seq, seq, pl.BlockSpec((l // tk, ATT_HEAD_DIM, tk), lambda i, j: (i, 0, 0))],
        out_specs=pl.BlockSpec((1, CHUNK, nq), lambda i, j: (i, j, 0)),
        scratch_shapes=[pltpu.VMEM((l, LANES), I32), pltpu.VMEM((SUBLANES, LANES), I32),
                        pltpu.VMEM((1, 2 * rows), F32), pltpu.VMEM((1, 2 * rows), F32),
                        pltpu.VMEM((ATT_HEAD_DIM, 2 * rows), F32),
                        pltpu.VMEM((tk, 2 * rows), F32), pltpu.VMEM((tk, 2 * rows), BF16)],
        compiler_params=_params("parallel", "arbitrary"),
        name="dsa",
    )(qe, qo, qie, qio, wt, ka, kb, vt)


def _split_cols(w, sizes):
    out, c0 = [], 0
    for s in sizes:
        out.append(w[:, c0:c0 + s])
        c0 += s
    return out


def _pad_cols(w, width):
    return jnp.pad(w, ((0, 0), (0, width - w.shape[1])))


def _pad_row(v, width):
    return jnp.pad(v, (0, width - v.shape[0]))[None, :]


def _hybrid_mixer_parts(x2d, b, l, mix_norm, w_in, conv_w, conv_b, dt_bias, a_log, d_skip, ssd_norm,
                        igate_bias, fgate_bias, mlstm_norm, *, tm):
    wz, wxbc, wdt, wq, wk, wv, wo, wi, wf = _split_cols(w_in, HY_SPLITS)
    w = jnp.concatenate([wz, wxbc, wq, wk, wv, wo, _pad_cols(wdt, LANES), _pad_cols(wi, LANES),
                         _pad_cols(wf, LANES)], axis=1).astype(BF16)
    z, xc, qm, km, vm, om, dts, ig, fg = _hy_proj(
        x2d, mix_norm[None, :], w, conv_w, conv_b[None, :], _pad_row(dt_bias, LANES), _pad_row(igate_bias, LANES),
        _pad_row(fgate_bias, LANES), l, tm=tm)
    acts = [a.reshape(b, l, a.shape[1]) for a in (z, xc, dts, qm, km, vm, om, ig, fg)]
    params = [_pad_row(a_log, LANES), jnp.repeat(d_skip, SSD_HEAD_DIM)[None, :], ssd_norm[None, :],
              mlstm_norm[None, :]]
    y, hm = _mixer(acts, params)
    return y.reshape(b * l, SSD_WIDTH), hm.reshape(b * l, ML_WIDTH)


def _rope_tables(positions):
    half = ATT_HEAD_DIM // 2
    inv = ROPE_THETA ** (-jnp.arange(0, ATT_HEAD_DIM, 2, dtype=F32) / ATT_HEAD_DIM)
    ang = positions.astype(F32)[..., None] * inv
    cos, sin = jnp.cos(ang), jnp.sin(ang)
    reps = LANES // half
    cos_t = jnp.tile(cos, (1, 1, reps))
    sin_t = jnp.tile(jnp.concatenate([-sin, sin], axis=-1), (1, 1, reps // 2))
    t = positions.shape[0] * positions.shape[1]
    return cos_t.reshape(t, LANES), sin_t.reshape(t, LANES)


def _sparse_attention_parts(x2d, b, l, cos, sin, mix_norm, w_in, q_norm, k_norm, topk, *, tm):
    wq, wk, wv, wqi, wki, wwi = _split_cols(w_in, SA_SPLITS)
    w = jnp.concatenate([wq, wqi, wk, wki, _pad_cols(wv, LANES), _pad_cols(wwi, LANES)], axis=1).astype(BF16)
    gq = jnp.tile(q_norm, ATT_HEADS)[None, :]
    gk = _pad_row(k_norm, LANES)
    qe, qo, qie, qio, ka, kb, vt, wt = _sa_proj(x2d, mix_norm[None, :], w, gq, gk, cos, sin, tm=tm, tk=ATT_KEY_CHUNK)
    r3 = lambda a: a.reshape(b, l, a.shape[1])
    o = _dsa(qe, qo, qie, qio, wt, r3(ka), r3(kb), vt, topk=topk)
    return o.reshape(b * l, ATT_HEADS * ATT_HEAD_DIM)


def kernel(x, positions, ffn_norm, ffn_w_gate, ffn_w_up, ffn_w_down, mix_norm, hy_w_in, hy_conv_w, hy_conv_b,
           hy_dt_bias, hy_a_log, hy_d_skip, hy_ssd_norm, hy_igate_bias, hy_fgate_bias, hy_mlstm_norm, hy_w_out,
           sa_w_in, sa_q_norm, sa_k_norm, sa_w_out):
    b, l, d = x.shape
    depth = ffn_norm.shape[0]
    t = b * l
    tm = 512 if t % 512 == 0 else CHUNK
    topk = min(TOPK_MAX, l // 4)
    cos, sin = _rope_tables(positions)
    wg, wu, wd = ffn_w_gate.astype(BF16), ffn_w_up.astype(BF16), ffn_w_down.astype(BF16)
    x2d = x.reshape(t, d)
    for layer in range(depth):
        x2d = _ffn(x2d, ffn_norm[layer, 0][None, :], wg, wu, wd, layer, 0, tm=tm)
        if layer % 2 == 0:
            e = layer // 2
            y, hm = _hybrid_mixer_parts(x2d, b, l, mix_norm[layer], hy_w_in[e], hy_conv_w[e], hy_conv_b[e],
                                        hy_dt_bias[e], hy_a_log[e], hy_d_skip[e], hy_ssd_norm[e],
                                        hy_igate_bias[e], hy_fgate_bias[e], hy_mlstm_norm[e], tm=tm)
            w_out = hy_w_out[e].astype(BF16)
            acts, ws = [y, hm], [w_out[:SSD_WIDTH], w_out[SSD_WIDTH:]]
        else:
            o = layer // 2
            att = _sparse_attention_parts(x2d, b, l, cos, sin, mix_norm[layer], sa_w_in[o], sa_q_norm[o],
                                          sa_k_norm[o], topk, tm=tm)
            acts, ws = [att], [sa_w_out[o].astype(BF16)]
        x2d = _proj_ffn(x2d, acts, ws, ffn_norm[layer, 1][None, :], wg, wu, wd, layer, 1, tm=tm)
    return x2d.reshape(b, l, d)
```

```python
import functools

import jax
import jax.numpy as jnp
import numpy as np
from jax import lax
from jax.experimental import pallas as pl
from jax.experimental.pallas import tpu as pltpu

F32 = jnp.float32
BF16 = jnp.bfloat16
I32 = jnp.int32

EPS = 1e-6
ROPE_THETA = 10000.0
LOG2_E = 1.4426950408889634
LANES = 128
SUBLANES = 8
CHUNK = 128
CONV_K = 4
TOPK_MAX = 256

D_MODEL = 1024
D_FF = 2816
SSD_WIDTH = 1024
SSD_HEAD_DIM = 64
SSD_HEADS = 16
SSD_GROUPS = 2
SSD_STATE = 128
SSD_CONV_DIM = SSD_WIDTH + 2 * SSD_GROUPS * SSD_STATE
ML_WIDTH = 1024
ML_HEADS = 4
ML_V_DIM = 256
ML_QK_DIM = 128
HY_SPLITS = (SSD_WIDTH, SSD_CONV_DIM, SSD_HEADS, ML_HEADS * ML_QK_DIM, ML_HEADS * ML_QK_DIM,
             ML_WIDTH, ML_WIDTH, ML_HEADS, ML_HEADS)
ATT_HEAD_DIM = 64
ATT_HEADS = 16
IDX_HEADS = 8
IDX_DIM = 64
SA_SPLITS = (ATT_HEADS * ATT_HEAD_DIM, ATT_HEAD_DIM, ATT_HEAD_DIM, IDX_HEADS * IDX_DIM, IDX_DIM, IDX_HEADS)

VMEM_LIMIT_BYTES = 56 * 1024 * 1024

NT_DIMS = (((1,), (1,)), ((), ()))


def _params(*sem):
    return pltpu.CompilerParams(dimension_semantics=sem, vmem_limit_bytes=VMEM_LIMIT_BYTES)


def _resident(shape):
    nd = len(shape)
    return pl.BlockSpec(shape, lambda *_: (0,) * nd, pipeline_mode=pl.Buffered(1))


def _dot(a, b):
    return jnp.dot(a, b, preferred_element_type=F32)


def _dot_nt(a, b):
    return lax.dot_general(a, b, NT_DIMS, preferred_element_type=F32)


def _split3(a):
    a1 = a.astype(BF16)
    r = a - a1.astype(F32)
    a2 = r.astype(BF16)
    r = r - a2.astype(F32)
    return a1, a2, r.astype(BF16)


N_SPLIT = 3


def _dot_f32_lhs(a, b01_stack):
    return _dot(jnp.concatenate(_split3(a), axis=1), b01_stack)


def _dot_f32_rhs(a01_stack, b):
    return _dot(a01_stack, jnp.concatenate(_split3(b), axis=0))


def _rms(x, g):
    return x * lax.rsqrt(jnp.mean(x * x, axis=-1, keepdims=True) + EPS) * g


def _sigmoid(x):
    return 1.0 / (1.0 + jnp.exp(-x))


def _softplus(x):
    return jnp.maximum(x, 0.0) + jnp.log1p(jnp.exp(-jnp.abs(x)))


def _iota(shape, dim):
    return lax.broadcasted_iota(I32, shape, dim)


def _expander(rows, cols, width):
    return np.arange(cols)[None, :] // width == np.arange(rows)[:, None]


def _lhs_stack(b01):
    return jnp.asarray(np.tile(b01, (N_SPLIT, 1)), dtype=BF16)


def _rhs_stack(a01):
    return jnp.asarray(np.tile(a01, (1, N_SPLIT)), dtype=BF16)


def _stacked(w, layer, idx):
    return pl.BlockSpec((None, None) + w.shape[2:], lambda *_: (layer, idx, 0, 0), pipeline_mode=pl.Buffered(1))


def _tril(n):
    return (_iota((n, n), 0) >= _iota((n, n), 1))


def _swiglu_half(x, g_ref, wg_ref, wu_ref, wd_ref, ff_chunk):
    h = _rms(x, g_ref[...]).astype(BF16)
    acc = jnp.zeros(x.shape, F32)
    for c0 in range(0, wg_ref.shape[1], ff_chunk):
        gate = _dot(h, wg_ref[:, c0:c0 + ff_chunk].astype(BF16))
        up = _dot(h, wu_ref[:, c0:c0 + ff_chunk].astype(BF16))
        act = (gate * _sigmoid(gate) * up).astype(BF16)
        acc = acc + _dot(act, wd_ref[c0:c0 + ff_chunk, :].astype(BF16))
    return x + 0.5 * acc


def _ffn_body(x_ref, g_ref, wg_ref, wu_ref, wd_ref, o_ref, *, ff_chunk):
    o_ref[...] = _swiglu_half(x_ref[...], g_ref, wg_ref, wu_ref, wd_ref, ff_chunk)


def _ffn(x2d, g, wg, wu, wd, layer, idx, *, tm, ff_chunk=256):
    t, d = x2d.shape
    return pl.pallas_call(
        functools.partial(_ffn_body, ff_chunk=ff_chunk),
        out_shape=jax.ShapeDtypeStruct((t, d), F32),
        grid=(t // tm,),
        in_specs=[pl.BlockSpec((tm, d), lambda i: (i, 0)), _resident(g.shape),
                  _stacked(wg, layer, idx), _stacked(wu, layer, idx), _stacked(wd, layer, idx)],
        out_specs=pl.BlockSpec((tm, d), lambda i: (i, 0)),
        compiler_params=_params("parallel"),
        name="ffn",
    )(x2d, g, wg, wu, wd)


def _proj_ffn_body(*refs, n_in, ff_chunk):
    x_ref = refs[0]
    a_refs = refs[1:1 + n_in]
    w_refs = refs[1 + n_in:1 + 2 * n_in]
    g_ref, wg_ref, wu_ref, wd_ref, o_ref = refs[1 + 2 * n_in:]
    x = x_ref[...]
    for a_ref, w_ref in zip(a_refs, w_refs):
        x = x + _dot(a_ref[...], w_ref[...])
    o_ref[...] = _swiglu_half(x, g_ref, wg_ref, wu_ref, wd_ref, ff_chunk)


def _proj_ffn(x2d, acts, ws, g, wg, wu, wd, layer, idx, *, tm, ff_chunk=256):
    t, d = x2d.shape
    n_in = len(acts)
    return pl.pallas_call(
        functools.partial(_proj_ffn_body, n_in=n_in, ff_chunk=ff_chunk),
        out_shape=jax.ShapeDtypeStruct((t, d), F32),
        grid=(t // tm,),
        in_specs=([pl.BlockSpec((tm, d), lambda i: (i, 0))]
                  + [pl.BlockSpec((tm, a.shape[1]), lambda i: (i, 0)) for a in acts]
                  + [_resident(w.shape) for w in ws]
                  + [_resident(g.shape), _stacked(wg, layer, idx), _stacked(wu, layer, idx),
                     _stacked(wd, layer, idx)]),
        out_specs=pl.BlockSpec((tm, d), lambda i: (i, 0)),
        compiler_params=_params("parallel"),
        name="proj_ffn",
    )(x2d, *acts, *ws, g, wg, wu, wd)


def _hy_proj_body(x_ref, g_ref, w_ref, cw_ref, cb_ref, dtb_ref, ib_ref, fb_ref, z_ref, xc_ref, q_ref, k_ref, v_ref,
                  o_ref, dt_ref, i_ref, f_ref, tails, *, tiles_per_seq):
    tm = x_ref.shape[0]
    lane = _iota((tm, LANES), 1)
    ssd_head, ml_head = lane < SSD_HEADS, lane < ML_HEADS
    same = lambda a: a
    outputs = (
        (z_ref, lambda a: a * _sigmoid(a)),
        (None, None),
        (q_ref, lambda a: a * (ML_QK_DIM ** -0.5)),
        (k_ref, same),
        (v_ref, same),
        (o_ref, _sigmoid),
        (dt_ref, lambda a: jnp.where(ssd_head, _softplus(a + dtb_ref[...]), 0.0)),
        (i_ref, lambda a: jnp.where(ml_head, a + ib_ref[...], 0.0)),
        (f_ref, lambda a: jnp.where(ml_head, -_softplus(-(a + fb_ref[...])), 0.0)),
    )

    @pl.when(pl.program_id(0) % tiles_per_seq == 0)
    def _():
        tails[...] = jnp.zeros(tails.shape, F32)

    h = _rms(x_ref[...], g_ref[...]).astype(BF16)

    u = _dot(h, w_ref[:, SSD_WIDTH:SSD_WIDTH + SSD_CONV_DIM])
    first_row = _iota((SUBLANES, SSD_CONV_DIM), 0) == 0
    part = cw_ref[0:1, :] * u
    for k in range(1, CONV_K):
        rolled = pltpu.roll(part, 1, 0)
        head = jnp.where(first_row, pltpu.roll(tails[k - 1], 1, 0), rolled[:SUBLANES])
        tails[k - 1] = part[tm - SUBLANES:, :]
        part = jnp.concatenate([head, rolled[SUBLANES:]], axis=0) + cw_ref[k:k + 1, :] * u
    conv = part + cb_ref[...]
    xc_ref[...] = (conv * _sigmoid(conv)).astype(xc_ref.dtype)

    c0 = 0
    for out_ref, finish in outputs:
        if out_ref is None:
            c0 += SSD_CONV_DIM
            continue
        wd = out_ref.shape[1]
        out_ref[...] = finish(_dot(h, w_ref[:, c0:c0 + wd])).astype(out_ref.dtype)
        c0 += wd


def _hy_proj(x2d, g, w, conv_w, conv_b, dt_bias, ib, fb, seq_len, *, tm):
    t, d = x2d.shape
    kw = ML_HEADS * ML_QK_DIM
    widths = (SSD_WIDTH, SSD_CONV_DIM, kw, kw, ML_WIDTH, ML_WIDTH, LANES, LANES, LANES)
    return pl.pallas_call(
        functools.partial(_hy_proj_body, tiles_per_seq=seq_len // tm),
        out_shape=[jax.ShapeDtypeStruct((t, wd), F32) for wd in widths],
        grid=(t // tm,),
        in_specs=[pl.BlockSpec((tm, d), lambda i: (i, 0)), _resident(g.shape), _resident(w.shape),
                  _resident(conv_w.shape), _resident(conv_b.shape), _resident(dt_bias.shape),
                  _resident(ib.shape), _resident(fb.shape)],
        out_specs=[pl.BlockSpec((tm, wd), lambda i: (i, 0)) for wd in widths],
        scratch_shapes=[pltpu.VMEM((CONV_K - 1, SUBLANES, SSD_CONV_DIM), F32)],
        compiler_params=_params("arbitrary"),
        name="hy_proj",
    )(x2d, g, w, conv_w, conv_b, dt_bias, ib, fb)


def _ssd_body(zg_ref, xc_ref, dt_ref, alog_ref, dskip_ref, nrm_ref, tril_ref, expand_ref, y_ref, state):
    q_len = CHUNK
    hp = SSD_WIDTH // SSD_GROUPS
    xc = xc_ref[0]
    xs = xc[:, :SSD_WIDTH]
    bm = xc[:, SSD_WIDTH:SSD_WIDTH + SSD_GROUPS * SSD_STATE]
    cm = xc[:, SSD_WIDTH + SSD_GROUPS * SSD_STATE:]

    dt = dt_ref[0]
    a = dt * (-jnp.exp(alog_ref[...]))
    tril = _tril(q_len)
    acs = _dot_f32_rhs(tril_ref[...], a)
    eacs = jnp.exp(acs)
    dend = jnp.exp(acs[q_len - 1:q_len, :] - acs)
    wide = _dot_f32_lhs(jnp.concatenate([dt, eacs, dend], axis=0), expand_ref[...])
    dt_x, eacs_x, dend_x = wide[:q_len], wide[q_len:2 * q_len], wide[2 * q_len:]

    xd = xs * dt_x
    lane_w = _iota((q_len, SSD_WIDTH), 1)
    low_half = (lane_w % LANES) < SSD_HEAD_DIM
    xd_halves = (jnp.where(low_half, xd, 0.0).astype(BF16), jnp.where(low_half, 0.0, xd).astype(BF16))
    xdd_b = (xd * dend_x).astype(BF16)
    acs_t = acs.T
    st = state[...]
    st_b = st.astype(BF16)

    y_diag, y_off, s_new = [], [], []
    heads_per_group = SSD_HEADS // SSD_GROUPS
    for g in range(SSD_GROUPS):
        bg = bm[:, g * SSD_STATE:(g + 1) * SSD_STATE]
        cg_b = cm[:, g * SSD_STATE:(g + 1) * SSD_STATE].astype(BF16)
        cb = _dot_nt(cg_b, bg.astype(BF16))
        s_new.append(_dot(bg.T.astype(BF16), xdd_b[:, g * hp:(g + 1) * hp]))
        y_off.append(_dot(cg_b, st_b[:, g * hp:(g + 1) * hp]))
        for pair in range(heads_per_group // 2):
            acc = None
            for par in range(2):
                h = g * heads_per_group + 2 * pair + par
                seg = acs[:, h:h + 1] - acs_t[h:h + 1, :]
                m = (cb * jnp.exp(jnp.where(tril, seg, -jnp.inf))).astype(BF16)
                slab = (h // 2) * LANES
                d = _dot(m, xd_halves[par][:, slab:slab + LANES])
                acc = d if acc is None else acc + d
            y_diag.append(acc)
    y = (jnp.concatenate(y_diag, axis=1) + eacs_x * jnp.concatenate(y_off, axis=1) + xs * dskip_ref[...])
    state[...] = st * eacs_x[q_len - 1:q_len, :] + jnp.concatenate(s_new, axis=1)

    yg = y * zg_ref[0]
    nrm = nrm_ref[...]
    y_ref[0] = jnp.concatenate(
        [_rms(yg[:, g * hp:(g + 1) * hp], nrm[:, g * hp:(g + 1) * hp]) for g in range(SSD_GROUPS)],
        axis=1).astype(y_ref.dtype)


def _mlstm_body(q_ref, k_ref, v_ref, og_ref, i_ref, f_ref, nrm_ref, tril_ref, exp_v_ref, exp_k_ref,
                h_ref, c_st, n_st, m_st):
    q_len = CHUNK

    li = i_ref[0]
    lf = f_ref[0]
    tril = _tril(q_len)
    bcs = _dot_f32_rhs(tril_ref[...], lf)
    b_last = bcs[q_len - 1:q_len, :]
    gl = b_last - bcs + li
    m_loc = jnp.max(gl, axis=0, keepdims=True)
    wg = jnp.exp(gl - m_loc)
    m_in = m_st[0:1, :]
    m_new = jnp.maximum(b_last + m_in, m_loc)
    a_sc = jnp.exp(b_last + m_in - m_new)
    g_sc = jnp.exp(m_loc - m_new)

    sc = jnp.concatenate([wg, jnp.broadcast_to(a_sc, (SUBLANES, LANES)),
                          jnp.broadcast_to(g_sc, (SUBLANES, LANES))], axis=0)
    sc_v = _dot_f32_lhs(sc, exp_v_ref[...])
    sc_k = _dot_f32_lhs(sc, exp_k_ref[...])
    wg_v, a_v, g_v = sc_v[:q_len], sc_v[q_len:q_len + 1], sc_v[q_len + SUBLANES:q_len + SUBLANES + 1]
    wg_k, a_k, g_k = sc_k[:q_len], sc_k[q_len:q_len + 1], sc_k[q_len + SUBLANES:q_len + SUBLANES + 1]

    q = q_ref[0]
    k = k_ref[0]
    v = v_ref[0]
    q_b, k_b, v_b = q.astype(BF16), k.astype(BF16), v.astype(BF16)
    vw_b = (v * wg_v).astype(BF16)
    n_in = n_st[0:1, :]
    n_loc = jnp.sum(k * wg_k, axis=0, keepdims=True)
    d_t = (li - bcs).T

    outs, c_new = [], []
    for h in range(ML_HEADS):
        ks = slice(h * ML_QK_DIM, (h + 1) * ML_QK_DIM)
        vs = slice(h * ML_V_DIM, (h + 1) * ML_V_DIM)
        c_new.append(_dot(k[:, ks].T.astype(BF16), vw_b[:, vs]))
        dm = jnp.where(tril, bcs[:, h:h + 1] + d_t[h:h + 1, :], -jnp.inf)
        inter = bcs[:, h:h + 1] + m_in[:, h:h + 1]
        m_t = jnp.maximum(jnp.max(dm, axis=1, keepdims=True), inter)
        s = _dot_nt(q_b[:, ks], k_b[:, ks]) * jnp.exp(dm - m_t)
        w_int = jnp.exp(inter - m_t)
        num = _dot(s.astype(BF16), v_b[:, vs]) + w_int * _dot(q_b[:, ks], c_st[:, vs].astype(BF16))
        den = (jnp.sum(s, axis=1, keepdims=True)
               + w_int * jnp.sum(q[:, ks] * n_in[:, ks], axis=1, keepdims=True))
        outs.append(num / jnp.maximum(jnp.abs(den), jnp.exp(-m_t)))
    nrm = nrm_ref[...]
    hn = jnp.concatenate([_rms(outs[h], nrm[:, h * ML_V_DIM:(h + 1) * ML_V_DIM]) for h in range(ML_HEADS)],
                         axis=1)
    h_ref[0] = (hn * og_ref[0]).astype(h_ref.dtype)

    c_st[...] = a_v * c_st[...] + g_v * jnp.concatenate(c_new, axis=1)
    n_st[...] = jnp.broadcast_to(a_k * n_in + g_k * n_loc, n_st.shape)
    m_st[...] = jnp.broadcast_to(m_new, m_st.shape)


MIXER_SEQS_PER_STEP = 4


def _mixer_body(z_ref, xc_ref, dt_ref, q_ref, k_ref, v_ref, o_ref, i_ref, f_ref,
                alog_ref, dskip_ref, snrm_ref, mnrm_ref,
                tril_ref, exp_h_ref, exp_v_ref, exp_k_ref, y_ref, h_ref, state, c_st, n_st, m_st):
    @pl.when(pl.program_id(1) == 0)
    def _():
        for st_ref in (state, c_st, n_st, m_st):
            st_ref[...] = jnp.zeros(st_ref.shape, F32)

    for bi in range(z_ref.shape[0]):
        one = lambda ref: ref.at[bi:bi + 1]
        _ssd_body(one(z_ref), one(xc_ref), one(dt_ref), alog_ref, dskip_ref, snrm_ref, tril_ref, exp_h_ref,
                  one(y_ref), state.at[bi])
        _mlstm_body(one(q_ref), one(k_ref), one(v_ref), one(o_ref), one(i_ref), one(f_ref), mnrm_ref,
                    tril_ref, exp_v_ref, exp_k_ref, one(h_ref), c_st.at[bi], n_st.at[bi], m_st.at[bi])


def _mixer(acts, params):
    b, l, _ = acts[0].shape
    kw = ML_HEADS * ML_QK_DIM
    consts = [_rhs_stack(np.tril(np.ones((CHUNK, CHUNK)))),
              _lhs_stack(_expander(LANES, SSD_WIDTH, SSD_HEAD_DIM)), _lhs_stack(_expander(LANES, ML_WIDTH, ML_V_DIM)),
              _lhs_stack(_expander(LANES, kw, ML_QK_DIM))]
    nb = MIXER_SEQS_PER_STEP if b % MIXER_SEQS_PER_STEP == 0 else 1
    blk = lambda w: pl.BlockSpec((nb, CHUNK, w), lambda i, c: (i, c, 0))
    return pl.pallas_call(
        _mixer_body,
        out_shape=[jax.ShapeDtypeStruct((b, l, SSD_WIDTH), BF16), jax.ShapeDtypeStruct((b, l, ML_WIDTH), BF16)],
        grid=(b // nb, l // CHUNK),
        in_specs=([blk(a.shape[2]) for a in acts] + [_resident(p.shape) for p in params]
                  + [_resident(c.shape) for c in consts]),
        out_specs=[blk(SSD_WIDTH), blk(ML_WIDTH)],
        scratch_shapes=[pltpu.VMEM((nb, SSD_STATE, SSD_WIDTH), F32),
                        pltpu.VMEM((nb, ML_QK_DIM, ML_WIDTH), F32),
                        pltpu.VMEM((nb, SUBLANES, kw), F32),
                        pltpu.VMEM((nb, SUBLANES, LANES), F32)],
        compiler_params=_params("parallel", "arbitrary"),
        name="mixer",
    )(*acts, *params, *consts)


def _rope(x, cos, sin, first_half):
    partner = jnp.where(first_half, pltpu.roll(x, LANES - ATT_HEAD_DIM // 2, 1), pltpu.roll(x, ATT_HEAD_DIM // 2, 1))
    return x * cos + partner * sin


def _sa_proj_body(x_ref, g_ref, w_ref, gq_ref, gk_ref, cos_ref, sin_ref, seg_ref, expand_ref,
                  qe_ref, qo_ref, qie_ref, qio_ref, ka_ref, kb_ref, vt_ref, wt_ref):
    tm = x_ref.shape[0]
    tk = vt_ref.shape[2]
    nq = ATT_HEADS * ATT_HEAD_DIM
    nqi = IDX_HEADS * IDX_DIM
    h = _rms(x_ref[...], g_ref[...]).astype(BF16)
    proj = _dot(h, w_ref[...])
    cos, sin = cos_ref[...], sin_ref[...]
    lane = _iota((tm, LANES), 1)
    first_half = (lane % ATT_HEAD_DIM) < ATT_HEAD_DIM // 2
    low_head = lane < ATT_HEAD_DIM

    q = proj[:, :nq]
    ms = _dot_f32_lhs(q * q, seg_ref[...]) * (1.0 / ATT_HEAD_DIM)
    rs = _dot_f32_lhs(lax.rsqrt(ms + EPS), expand_ref[...])
    qn = q * rs * gq_ref[...]
    q_scale = ATT_HEAD_DIM ** -0.5 * LOG2_E

    def emit(even_ref, odd_ref, slab, s0):
        low = _iota((CHUNK, LANES), 1) < ATT_HEAD_DIM
        for b0 in range(0, tm, CHUNK):
            rows = slab[b0:b0 + CHUNK]
            even_ref[b0 // CHUNK, :, s0:s0 + LANES] = jnp.where(low, rows, 0.0).T.astype(BF16)
            odd_ref[b0 // CHUNK, :, s0:s0 + LANES] = jnp.where(low, 0.0, rows).T.astype(BF16)

    for s0 in range(0, nq, LANES):
        emit(qe_ref, qo_ref, _rope(qn[:, s0:s0 + LANES], cos, sin, first_half) * q_scale, s0)
    for s0 in range(0, nqi, LANES):
        emit(qie_ref, qio_ref, _rope(proj[:, nq + s0:nq + s0 + LANES], cos, sin, first_half), s0)

    kk = proj[:, nq + nqi:nq + nqi + LANES]
    k_ms = jnp.sum(jnp.where(low_head, kk * kk, 0.0), axis=1, keepdims=True) * (1.0 / ATT_HEAD_DIM)
    kk = jnp.where(low_head, kk * lax.rsqrt(k_ms + EPS) * gk_ref[...], kk)
    kk = _rope(kk, cos, sin, first_half)
    ka_ref[...] = kk.astype(BF16)
    kb_ref[...] = pltpu.roll(kk, ATT_HEAD_DIM, 1).astype(BF16)
    vv = proj[:, nq + nqi + LANES:nq + nqi + 2 * LANES]
    for c in range(tm // tk):
        vt_ref[c] = vv[c * tk:(c + 1) * tk, :].T[:ATT_HEAD_DIM, :].astype(BF16)
    wt_ref[...] = proj[:, nq + nqi + 2 * LANES:].T[:SUBLANES, :]


def _sa_proj(x2d, g, w, gq, gk, cos, sin, *, tm, tk):
    t, d = x2d.shape
    nq = ATT_HEADS * ATT_HEAD_DIM
    nqi = IDX_HEADS * IDX_DIM
    row = lambda wd: pl.BlockSpec((tm, wd), lambda i: (i, 0))
    qblocks = lambda wd: jax.ShapeDtypeStruct((t // CHUNK, LANES, wd), BF16)
    qspec = lambda wd: pl.BlockSpec((tm // CHUNK, LANES, wd), lambda i: (i, 0, 0))
    outs = [qblocks(nq), qblocks(nq), qblocks(nqi), qblocks(nqi),
            jax.ShapeDtypeStruct((t, LANES), BF16), jax.ShapeDtypeStruct((t, LANES), BF16),
            jax.ShapeDtypeStruct((t // tk, ATT_HEAD_DIM, tk), BF16),
            jax.ShapeDtypeStruct((SUBLANES, t), F32)]
    head_of_lane = _expander(LANES, nq, ATT_HEAD_DIM)
    expand, seg = _lhs_stack(head_of_lane), _lhs_stack(head_of_lane.T)
    return pl.pallas_call(
        _sa_proj_body,
        out_shape=outs,
        grid=(t // tm,),
        in_specs=[row(d), _resident(g.shape), _resident(w.shape), _resident(gq.shape), _resident(gk.shape),
                  row(LANES), row(LANES), _resident(seg.shape), _resident(expand.shape)],
        out_specs=[qspec(nq), qspec(nq), qspec(nqi), qspec(nqi), row(LANES), row(LANES),
                   pl.BlockSpec((tm // tk, ATT_HEAD_DIM, tk), lambda i: (i, 0, 0)),
                   pl.BlockSpec((SUBLANES, tm), lambda i: (0, i))],
        compiler_params=_params("parallel"),
        name="sa_proj",
    )(x2d, g, w, gq, gk, cos, sin, seg, expand)


INT_MIN = -2 ** 31
NEG_INF_KEY = INT_MIN + 0x7FFFFF
ATT_KEY_CHUNK = 256


def _sortable_key(x):
    bits = pltpu.bitcast(jnp.where(x == 0.0, 0.0, x), I32)
    return bits ^ ((bits >> 31) & 0x7FFFFFFF)


def _dsa_body(qe_ref, qo_ref, qie_ref, qio_ref, wt_ref, ka_ref, kb_ref, vt_ref, o_ref,
              skey, last_s, m_s, l_s, a_s, acc_s, p_s, *, topk, seq_bits):
    blk = CHUNK
    tk = vt_ref.shape[2]
    j = pl.program_id(1)
    n_chunks = lax.div(j * blk + blk + tk - 1, tk)
    row = _iota((tk, LANES), 0)
    q_pos = j * blk + _iota((tk, LANES), 1)

    qe, qo, qie, qio = qe_ref.at[0], qo_ref.at[0], qie_ref.at[0], qio_ref.at[0]
    n_pairs = ATT_HEADS // 2
    n_ipairs = IDX_HEADS // 2

    idx_scale = (IDX_HEADS ** -0.5) * (IDX_DIM ** -0.5)
    wt = wt_ref[...]

    def chunk(ci):
        return pl.ds(pl.multiple_of(ci * tk, tk), tk)

    cw = 2 * LANES

    def score_chunk(ci):
        sc = jnp.zeros((tk, LANES), F32)
        for par, (src, k_ref) in enumerate(((qie, kb_ref), (qio, ka_ref))):
            kc = k_ref[0, chunk(ci), :]
            for c0 in range(0, n_ipairs * blk, cw):
                logits = _dot(kc, src[:, c0:c0 + cw])
                for u in range(cw // blk):
                    head = 2 * (c0 // blk + u) + par
                    sc = sc + jnp.maximum(logits[:, u * blk:(u + 1) * blk], 0.0) * wt[head:head + 1, :]
        sc = sc * idx_scale
        sc = jnp.where(ci * tk + row <= q_pos, sc, -jnp.inf)
        skey[chunk(ci), :] = _sortable_key(sc)

    def score_chunk_pair(i, _):
        score_chunk(2 * i)
        score_chunk(2 * i + 1)
        return 0

    lax.fori_loop(0, lax.shift_right_logical(n_chunks, 1), score_chunk_pair, 0)

    @pl.when((n_chunks & 1) == 1)
    def _():
        score_chunk(n_chunks - 1)

    acc_rows = 4 * SUBLANES

    def count(pred_fn):
        def body(ci, c):
            hit = jnp.where(pred_fn(skey[chunk(ci), :], ci), 1.0, 0.0)
            return c + jnp.sum(hit.reshape(tk // acc_rows, acc_rows, LANES), axis=0)
        part = lax.fori_loop(0, n_chunks, body, jnp.zeros((acc_rows, LANES), F32))
        return jnp.sum(part, axis=0, keepdims=True)

    kf = float(topk)
    thr = jnp.full((1, LANES), INT_MIN, I32)
    n_ge = jnp.full((1, LANES), 1.0, F32) * (n_chunks * tk).astype(F32)
    for bit in range(31, -1, -1):
        cand = thr + jnp.int32(INT_MIN if bit == 31 else 1 << bit)
        c = count(lambda key, ci, cand=cand: key >= cand)
        thr = jnp.where(c >= kf, cand, thr)
        n_ge = jnp.where(c >= kf, c, n_ge)

    last_s[...] = jnp.full(last_s.shape, 2 ** 31 - 1, I32)
    tied = (n_ge > kf) & (thr > NEG_INF_KEY)

    @pl.when(jnp.sum(jnp.where(tied, 1.0, 0.0)) > 0.0)
    def _():
        need = kf - count(lambda key, ci: key > thr)
        last = jnp.zeros((1, LANES), I32)
        for bit in range(seq_bits - 1, -1, -1):
            cand = last + jnp.int32(1 << bit)
            c = count(lambda key, ci, cand=cand: (key == thr) & (ci * tk + row < cand))
            last = jnp.where(c <= need - 1.0, cand, last)
        last_s[...] = jnp.broadcast_to(last, last_s.shape)

    last = last_s[0:1, :]

    m_s[...] = jnp.full(m_s.shape, -1e30, F32)
    l_s[...] = jnp.zeros(l_s.shape, F32)
    acc_s[...] = jnp.zeros(acc_s.shape, F32)
    half = n_pairs * blk

    def masked_logits(ci):
        key = skey[chunk(ci), :]
        pos = ci * tk + row
        sel = ((key > thr) | ((key == thr) & (pos <= last))) & (pos <= q_pos)
        bias = jnp.where(sel, 0.0, -jnp.inf)
        bias = jnp.concatenate([bias] * (cw // blk), axis=1)
        for src, k_ref, base in ((qe, ka_ref, 0), (qo, kb_ref, half)):
            kc = k_ref[0, chunk(ci), :]
            for c0 in range(0, half, cw):
                yield base + c0, _dot(kc, src[:, c0:c0 + cw]) + bias

    def probabilities(ci):
        m_chunk = [jnp.max(s, axis=0, keepdims=True) for _, s in masked_logits(ci)]
        m_prev = m_s[...]
        m_cur = jnp.maximum(m_prev, jnp.concatenate(m_chunk, axis=1))
        alpha = jnp.exp2(m_prev - m_cur)
        l_chunk = []
        for c0, s in masked_logits(ci):
            p = jnp.exp2(s - m_cur[:, c0:c0 + cw])
            l_chunk.append(jnp.sum(p, axis=0, keepdims=True))
            p_s[:, c0:c0 + cw] = p.astype(BF16)
        l_s[...] = alpha * l_s[...] + jnp.concatenate(l_chunk, axis=1)
        m_s[...] = m_cur
        a_s[...] = alpha

    def accumulate(ci):
        acc_s[...] = a_s[...] * acc_s[...] + _dot(vt_ref[ci], p_s[...])

    def attn_chunk(ci):
        accumulate(ci - 1)
        probabilities(ci)

    def attn_chunk_pair(i, _):
        attn_chunk(2 * i + 1)
        attn_chunk(2 * i + 2)
        return 0

    probabilities(jnp.int32(0))
    n_rest = n_chunks - 1
    lax.fori_loop(0, lax.shift_right_logical(n_rest, 1), attn_chunk_pair, 0)

    @pl.when((n_rest & 1) == 1)
    def _():
        attn_chunk(n_rest)

    accumulate(n_rest)

    out = acc_s[...] / l_s[...]
    for p in range(n_pairs):
        pair = jnp.concatenate([out[:, p * blk:(p + 1) * blk], out[:, half + p * blk:half + (p + 1) * blk]], axis=0)
        o_ref[0, :, p * LANES:(p + 1) * LANES] = pair.T.astype(o_ref.dtype)


def _dsa(qe, qo, qie, qio, wt, ka, kb, vt, *, topk):
    b, l, _ = ka.shape
    nb = l // CHUNK
    tk = vt.shape[2]
    rows = (ATT_HEADS // 2) * CHUNK
    nq = ATT_HEADS * ATT_HEAD_DIM
    qop = lambda a: pl.BlockSpec((1,) + a.shape[1:], lambda i, j: (i * nb + j, 0, 0))
    seq = pl.BlockSpec((1, l, LANES), lambda i, j: (i, 0, 0))
    return pl.pallas_call(
        functools.partial(_dsa_body, topk=topk, seq_bits=int(np.log2(l))),
        out_shape=jax.ShapeDtypeStruct((b, l, nq), BF16),
        grid=(b, nb),
        in_specs=[qop(qe), qop(qo), qop(qie), qop(qio),
                  pl.BlockSpec((SUBLANES, CHUNK), lambda i, j: (0, i * nb + j)),
                  seq, seq, pl.BlockSpec((l // tk, ATT_HEAD_DIM, tk), lambda i, j: (i, 0, 0))],
        out_specs=pl.BlockSpec((1, CHUNK, nq), lambda i, j: (i, j, 0)),
        scratch_shapes=[pltpu.VMEM((l, LANES), I32), pltpu.VMEM((SUBLANES, LANES), I32),
                        pltpu.VMEM((1, 2 * rows), F32), pltpu.VMEM((1, 2 * rows), F32),
                        pltpu.VMEM((1, 2 * rows), F32), pltpu.VMEM((ATT_HEAD_DIM, 2 * rows), F32),
                        pltpu.VMEM((tk, 2 * rows), BF16)],
        compiler_params=_params("parallel", "arbitrary"),
        name="dsa",
    )(qe, qo, qie, qio, wt, ka, kb, vt)


def _split_cols(w, sizes):
    out, c0 = [], 0
    for s in sizes:
        out.append(w[:, c0:c0 + s])
        c0 += s
    return out


def _pad_cols(w, width):
    return jnp.pad(w, ((0, 0), (0, width - w.shape[1])))


def _pad_row(v, width):
    return jnp.pad(v, (0, width - v.shape[0]))[None, :]


def _hybrid_mixer_parts(x2d, b, l, mix_norm, w_in, conv_w, conv_b, dt_bias, a_log, d_skip, ssd_norm,
                        igate_bias, fgate_bias, mlstm_norm, *, tm):
    wz, wxbc, wdt, wq, wk, wv, wo, wi, wf = _split_cols(w_in, HY_SPLITS)
    w = jnp.concatenate([wz, wxbc, wq, wk, wv, wo, _pad_cols(wdt, LANES), _pad_cols(wi, LANES),
                         _pad_cols(wf, LANES)], axis=1).astype(BF16)
    z, xc, qm, km, vm, om, dts, ig, fg = _hy_proj(
        x2d, mix_norm[None, :], w, conv_w, conv_b[None, :], _pad_row(dt_bias, LANES), _pad_row(igate_bias, LANES),
        _pad_row(fgate_bias, LANES), l, tm=tm)
    acts = [a.reshape(b, l, a.shape[1]) for a in (z, xc, dts, qm, km, vm, om, ig, fg)]
    params = [_pad_row(a_log, LANES), jnp.repeat(d_skip, SSD_HEAD_DIM)[None, :], ssd_norm[None, :],
              mlstm_norm[None, :]]
    y, hm = _mixer(acts, params)
    return y.reshape(b * l, SSD_WIDTH), hm.reshape(b * l, ML_WIDTH)


def _rope_tables(positions):
    half = ATT_HEAD_DIM // 2
    inv = ROPE_THETA ** (-jnp.arange(0, ATT_HEAD_DIM, 2, dtype=F32) / ATT_HEAD_DIM)
    ang = positions.astype(F32)[..., None] * inv
    cos, sin = jnp.cos(ang), jnp.sin(ang)
    reps = LANES // half
    cos_t = jnp.tile(cos, (1, 1, reps))
    sin_t = jnp.tile(jnp.concatenate([-sin, sin], axis=-1), (1, 1, reps // 2))
    t = positions.shape[0] * positions.shape[1]
    return cos_t.reshape(t, LANES), sin_t.reshape(t, LANES)


def _sparse_attention_parts(x2d, b, l, cos, sin, mix_norm, w_in, q_norm, k_norm, topk, *, tm):
    wq, wk, wv, wqi, wki, wwi = _split_cols(w_in, SA_SPLITS)
    w = jnp.concatenate([wq, wqi, wk, wki, _pad_cols(wv, LANES), _pad_cols(wwi, LANES)], axis=1).astype(BF16)
    gq = jnp.tile(q_norm, ATT_HEADS)[None, :]
    gk = _pad_row(k_norm, LANES)
    qe, qo, qie, qio, ka, kb, vt, wt = _sa_proj(x2d, mix_norm[None, :], w, gq, gk, cos, sin, tm=tm, tk=ATT_KEY_CHUNK)
    r3 = lambda a: a.reshape(b, l, a.shape[1])
    o = _dsa(qe, qo, qie, qio, wt, r3(ka), r3(kb), vt, topk=topk)
    return o.reshape(b * l, ATT_HEADS * ATT_HEAD_DIM)


def kernel(x, positions, ffn_norm, ffn_w_gate, ffn_w_up, ffn_w_down, mix_norm, hy_w_in, hy_conv_w, hy_conv_b,
           hy_dt_bias, hy_a_log, hy_d_skip, hy_ssd_norm, hy_igate_bias, hy_fgate_bias, hy_mlstm_norm, hy_w_out,
           sa_w_in, sa_q_norm, sa_k_norm, sa_w_out):
    b, l, d = x.shape
    depth = ffn_norm.shape[0]
    t = b * l
    tm = 512 if t % 512 == 0 else CHUNK
    topk = min(TOPK_MAX, l // 4)
    cos, sin = _rope_tables(positions)
    wg, wu, wd = ffn_w_gate, ffn_w_up, ffn_w_down
    x2d = x.reshape(t, d)
    for layer in range(depth):
        x2d = _ffn(x2d, ffn_norm[layer, 0][None, :], wg, wu, wd, layer, 0, tm=tm)
        if layer % 2 == 0:
            e = layer // 2
            y, hm = _hybrid_mixer_parts(x2d, b, l, mix_norm[layer], hy_w_in[e], hy_conv_w[e], hy_conv_b[e],
                                        hy_dt_bias[e], hy_a_log[e], hy_d_skip[e], hy_ssd_norm[e],
                                        hy_igate_bias[e], hy_fgate_bias[e], hy_mlstm_norm[e], tm=tm)
            w_out = hy_w_out[e].astype(BF16)
            acts, ws = [y, hm], [w_out[:SSD_WIDTH], w_out[SSD_WIDTH:]]
        else:
            o = layer // 2
            att = _sparse_attention_parts(x2d, b, l, cos, sin, mix_norm[layer], sa_w_in[o], sa_q_norm[o],
                                          sa_k_norm[o], topk, tm=tm)
            acts, ws = [att], [sa_w_out[o].astype(BF16)]
        x2d = _proj_ffn(x2d, acts, ws, ffn_norm[layer, 1][None, :], wg, wu, wd, layer, 1, tm=tm)
    return x2d.reshape(b, l, d)
```

```python
import functools

import jax
import jax.numpy as jnp
import numpy as np
from jax import lax
from jax.experimental import pallas as pl
from jax.experimental.pallas import tpu as pltpu

F32 = jnp.float32
BF16 = jnp.bfloat16
I32 = jnp.int32

EPS = 1e-6
ROPE_THETA = 10000.0
LOG2_E = 1.4426950408889634
LANES = 128
SUBLANES = 8
CHUNK = 128
CONV_K = 4
TOPK_MAX = 256

D_MODEL = 1024
D_FF = 2816
SSD_WIDTH = 1024
SSD_HEAD_DIM = 64
SSD_HEADS = 16
SSD_GROUPS = 2
SSD_STATE = 128
SSD_CONV_DIM = SSD_WIDTH + 2 * SSD_GROUPS * SSD_STATE
ML_WIDTH = 1024
ML_HEADS = 4
ML_V_DIM = 256
ML_QK_DIM = 128
HY_SPLITS = (SSD_WIDTH, SSD_CONV_DIM, SSD_HEADS, ML_HEADS * ML_QK_DIM, ML_HEADS * ML_QK_DIM,
             ML_WIDTH, ML_WIDTH, ML_HEADS, ML_HEADS)
ATT_HEAD_DIM = 64
ATT_HEADS = 16
IDX_HEADS = 8
IDX_DIM = 64
SA_SPLITS = (ATT_HEADS * ATT_HEAD_DIM, ATT_HEAD_DIM, ATT_HEAD_DIM, IDX_HEADS * IDX_DIM, IDX_DIM, IDX_HEADS)

VMEM_LIMIT_BYTES = 56 * 1024 * 1024

NT_DIMS = (((1,), (1,)), ((), ()))


def _params(*sem):
    return pltpu.CompilerParams(dimension_semantics=sem, vmem_limit_bytes=VMEM_LIMIT_BYTES)


def _resident(shape):
    nd = len(shape)
    return pl.BlockSpec(shape, lambda *_: (0,) * nd, pipeline_mode=pl.Buffered(1))


def _dot(a, b):
    return jnp.dot(a, b, preferred_element_type=F32)


def _dot_nt(a, b):
    return lax.dot_general(a, b, NT_DIMS, preferred_element_type=F32)


def _split3(a):
    a1 = a.astype(BF16)
    r = a - a1.astype(F32)
    a2 = r.astype(BF16)
    r = r - a2.astype(F32)
    return a1, a2, r.astype(BF16)


N_SPLIT = 3


def _dot_f32_lhs(a, b01_stack):
    return _dot(jnp.concatenate(_split3(a), axis=1), b01_stack)


def _dot_f32_rhs(a01_stack, b):
    return _dot(a01_stack, jnp.concatenate(_split3(b), axis=0))


def _rms(x, g):
    return x * lax.rsqrt(jnp.mean(x * x, axis=-1, keepdims=True) + EPS) * g


def _sigmoid(x):
    return 1.0 / (1.0 + jnp.exp(-x))


def _softplus(x):
    return jnp.maximum(x, 0.0) + jnp.log1p(jnp.exp(-jnp.abs(x)))


def _iota(shape, dim):
    return lax.broadcasted_iota(I32, shape, dim)


def _expander(rows, cols, width):
    return np.arange(cols)[None, :] // width == np.arange(rows)[:, None]


def _lhs_stack(b01):
    return jnp.asarray(np.tile(b01, (N_SPLIT, 1)), dtype=BF16)


def _rhs_stack(a01):
    return jnp.asarray(np.tile(a01, (1, N_SPLIT)), dtype=BF16)


def _stacked(w, layer, idx):
    return pl.BlockSpec((None, None) + w.shape[2:], lambda *_: (layer, idx, 0, 0), pipeline_mode=pl.Buffered(1))


def _tril(n):
    return (_iota((n, n), 0) >= _iota((n, n), 1))


def _swiglu_half(x, g_ref, wg_ref, wu_ref, wd_ref, ff_chunk):
    h = _rms(x, g_ref[...]).astype(BF16)
    acc = jnp.zeros(x.shape, F32)
    for c0 in range(0, wg_ref.shape[1], ff_chunk):
        gate = _dot(h, wg_ref[:, c0:c0 + ff_chunk].astype(BF16))
        up = _dot(h, wu_ref[:, c0:c0 + ff_chunk].astype(BF16))
        act = (gate * _sigmoid(gate) * up).astype(BF16)
        acc = acc + _dot(act, wd_ref[c0:c0 + ff_chunk, :].astype(BF16))
    return x + 0.5 * acc


def _ffn_body(x_ref, g_ref, wg_ref, wu_ref, wd_ref, o_ref, *, ff_chunk):
    o_ref[...] = _swiglu_half(x_ref[...], g_ref, wg_ref, wu_ref, wd_ref, ff_chunk)


def _ffn(x2d, g, wg, wu, wd, layer, idx, *, tm, ff_chunk=256):
    t, d = x2d.shape
    return pl.pallas_call(
        functools.partial(_ffn_body, ff_chunk=ff_chunk),
        out_shape=jax.ShapeDtypeStruct((t, d), F32),
        grid=(t // tm,),
        in_specs=[pl.BlockSpec((tm, d), lambda i: (i, 0)), _resident(g.shape),
                  _stacked(wg, layer, idx), _stacked(wu, layer, idx), _stacked(wd, layer, idx)],
        out_specs=pl.BlockSpec((tm, d), lambda i: (i, 0)),
        compiler_params=_params("parallel"),
        name="ffn",
    )(x2d, g, wg, wu, wd)


def _proj_ffn_body(*refs, n_in, ff_chunk):
    x_ref = refs[0]
    a_refs = refs[1:1 + n_in]
    w_refs = refs[1 + n_in:1 + 2 * n_in]
    g_ref, wg_ref, wu_ref, wd_ref, o_ref = refs[1 + 2 * n_in:]
    x = x_ref[...]
    for a_ref, w_ref in zip(a_refs, w_refs):
        x = x + _dot(a_ref[...], w_ref[...])
    o_ref[...] = _swiglu_half(x, g_ref, wg_ref, wu_ref, wd_ref, ff_chunk)


def _proj_ffn(x2d, acts, ws, g, wg, wu, wd, layer, idx, *, tm, ff_chunk=256):
    t, d = x2d.shape
    n_in = len(acts)
    return pl.pallas_call(
        functools.partial(_proj_ffn_body, n_in=n_in, ff_chunk=ff_chunk),
        out_shape=jax.ShapeDtypeStruct((t, d), F32),
        grid=(t // tm,),
        in_specs=([pl.BlockSpec((tm, d), lambda i: (i, 0))]
                  + [pl.BlockSpec((tm, a.shape[1]), lambda i: (i, 0)) for a in acts]
                  + [_resident(w.shape) for w in ws]
                  + [_resident(g.shape), _stacked(wg, layer, idx), _stacked(wu, layer, idx),
                     _stacked(wd, layer, idx)]),
        out_specs=pl.BlockSpec((tm, d), lambda i: (i, 0)),
        compiler_params=_params("parallel"),
        name="proj_ffn",
    )(x2d, *acts, *ws, g, wg, wu, wd)


def _hy_proj_body(x_ref, g_ref, w_ref, cw_ref, cb_ref, dtb_ref, ib_ref, fb_ref, z_ref, xc_ref, q_ref, k_ref, v_ref,
                  o_ref, dt_ref, i_ref, f_ref, tails, *, tiles_per_seq):
    tm = x_ref.shape[0]
    lane = _iota((tm, LANES), 1)
    ssd_head, ml_head = lane < SSD_HEADS, lane < ML_HEADS
    same = lambda a: a
    outputs = (
        (z_ref, lambda a: a * _sigmoid(a)),
        (None, None),
        (q_ref, lambda a: a * (ML_QK_DIM ** -0.5)),
        (k_ref, same),
        (v_ref, same),
        (o_ref, _sigmoid),
        (dt_ref, lambda a: jnp.where(ssd_head, _softplus(a + dtb_ref[...]), 0.0)),
        (i_ref, lambda a: jnp.where(ml_head, a + ib_ref[...], 0.0)),
        (f_ref, lambda a: jnp.where(ml_head, -_softplus(-(a + fb_ref[...])), 0.0)),
    )

    @pl.when(pl.program_id(0) % tiles_per_seq == 0)
    def _():
        tails[...] = jnp.zeros(tails.shape, F32)

    h = _rms(x_ref[...], g_ref[...]).astype(BF16)

    u = _dot(h, w_ref[:, SSD_WIDTH:SSD_WIDTH + SSD_CONV_DIM])
    first_row = _iota((SUBLANES, SSD_CONV_DIM), 0) == 0
    part = cw_ref[0:1, :] * u
    for k in range(1, CONV_K):
        rolled = pltpu.roll(part, 1, 0)
        head = jnp.where(first_row, pltpu.roll(tails[k - 1], 1, 0), rolled[:SUBLANES])
        tails[k - 1] = part[tm - SUBLANES:, :]
        part = jnp.concatenate([head, rolled[SUBLANES:]], axis=0) + cw_ref[k:k + 1, :] * u
    conv = part + cb_ref[...]
    xc_ref[...] = (conv * _sigmoid(conv)).astype(xc_ref.dtype)

    c0 = 0
    for out_ref, finish in outputs:
        if out_ref is None:
            c0 += SSD_CONV_DIM
            continue
        wd = out_ref.shape[1]
        out_ref[...] = finish(_dot(h, w_ref[:, c0:c0 + wd])).astype(out_ref.dtype)
        c0 += wd


def _hy_proj(x2d, g, w, conv_w, conv_b, dt_bias, ib, fb, seq_len, *, tm):
    t, d = x2d.shape
    kw = ML_HEADS * ML_QK_DIM
    widths = (SSD_WIDTH, SSD_CONV_DIM, kw, kw, ML_WIDTH, ML_WIDTH, LANES, LANES, LANES)
    return pl.pallas_call(
        functools.partial(_hy_proj_body, tiles_per_seq=seq_len // tm),
        out_shape=[jax.ShapeDtypeStruct((t, wd), F32) for wd in widths],
        grid=(t // tm,),
        in_specs=[pl.BlockSpec((tm, d), lambda i: (i, 0)), _resident(g.shape), _resident(w.shape),
                  _resident(conv_w.shape), _resident(conv_b.shape), _resident(dt_bias.shape),
                  _resident(ib.shape), _resident(fb.shape)],
        out_specs=[pl.BlockSpec((tm, wd), lambda i: (i, 0)) for wd in widths],
        scratch_shapes=[pltpu.VMEM((CONV_K - 1, SUBLANES, SSD_CONV_DIM), F32)],
        compiler_params=_params("arbitrary"),
        name="hy_proj",
    )(x2d, g, w, conv_w, conv_b, dt_bias, ib, fb)


def _ssd_body(zg_ref, xc_ref, dt_ref, alog_ref, dskip_ref, nrm_ref, tril_ref, expand_ref, y_ref, state):
    q_len = CHUNK
    hp = SSD_WIDTH // SSD_GROUPS
    xc = xc_ref[0]
    xs = xc[:, :SSD_WIDTH]
    bm = xc[:, SSD_WIDTH:SSD_WIDTH + SSD_GROUPS * SSD_STATE]
    cm = xc[:, SSD_WIDTH + SSD_GROUPS * SSD_STATE:]

    dt = dt_ref[0]
    a = dt * (-jnp.exp(alog_ref[...]))
    tril = _tril(q_len)
    acs = _dot_f32_rhs(tril_ref[...], a)
    eacs = jnp.exp(acs)
    dend = jnp.exp(acs[q_len - 1:q_len, :] - acs)
    wide = _dot_f32_lhs(jnp.concatenate([dt, eacs, dend], axis=0), expand_ref[...])
    dt_x, eacs_x, dend_x = wide[:q_len], wide[q_len:2 * q_len], wide[2 * q_len:]

    xd = xs * dt_x
    lane_w = _iota((q_len, SSD_WIDTH), 1)
    low_half = (lane_w % LANES) < SSD_HEAD_DIM
    xd_halves = (jnp.where(low_half, xd, 0.0).astype(BF16), jnp.where(low_half, 0.0, xd).astype(BF16))
    xdd_b = (xd * dend_x).astype(BF16)
    acs_t = acs.T
    st = state[...]
    st_b = st.astype(BF16)

    y_diag, y_off, s_new = [], [], []
    heads_per_group = SSD_HEADS // SSD_GROUPS
    for g in range(SSD_GROUPS):
        bg = bm[:, g * SSD_STATE:(g + 1) * SSD_STATE]
        cg_b = cm[:, g * SSD_STATE:(g + 1) * SSD_STATE].astype(BF16)
        cb = _dot_nt(cg_b, bg.astype(BF16))
        s_new.append(_dot(bg.T.astype(BF16), xdd_b[:, g * hp:(g + 1) * hp]))
        y_off.append(_dot(cg_b, st_b[:, g * hp:(g + 1) * hp]))
        for pair in range(heads_per_group // 2):
            acc = None
            for par in range(2):
                h = g * heads_per_group + 2 * pair + par
                seg = acs[:, h:h + 1] - acs_t[h:h + 1, :]
                m = (cb * jnp.exp(jnp.where(tril, seg, -jnp.inf))).astype(BF16)
                slab = (h // 2) * LANES
                d = _dot(m, xd_halves[par][:, slab:slab + LANES])
                acc = d if acc is None else acc + d
            y_diag.append(acc)
    y = (jnp.concatenate(y_diag, axis=1) + eacs_x * jnp.concatenate(y_off, axis=1) + xs * dskip_ref[...])
    state[...] = st * eacs_x[q_len - 1:q_len, :] + jnp.concatenate(s_new, axis=1)

    yg = y * zg_ref[0]
    nrm = nrm_ref[...]
    y_ref[0] = jnp.concatenate(
        [_rms(yg[:, g * hp:(g + 1) * hp], nrm[:, g * hp:(g + 1) * hp]) for g in range(SSD_GROUPS)],
        axis=1).astype(y_ref.dtype)


def _mlstm_body(q_ref, k_ref, v_ref, og_ref, i_ref, f_ref, nrm_ref, tril_ref, exp_v_ref, exp_k_ref,
                h_ref, c_st, n_st, m_st):
    q_len = CHUNK

    li = i_ref[0]
    lf = f_ref[0]
    tril = _tril(q_len)
    bcs = _dot_f32_rhs(tril_ref[...], lf)
    b_last = bcs[q_len - 1:q_len, :]
    gl = b_last - bcs + li
    m_loc = jnp.max(gl, axis=0, keepdims=True)
    wg = jnp.exp(gl - m_loc)
    m_in = m_st[0:1, :]
    m_new = jnp.maximum(b_last + m_in, m_loc)
    a_sc = jnp.exp(b_last + m_in - m_new)
    g_sc = jnp.exp(m_loc - m_new)

    sc = jnp.concatenate([wg, jnp.broadcast_to(a_sc, (SUBLANES, LANES)),
                          jnp.broadcast_to(g_sc, (SUBLANES, LANES))], axis=0)
    sc_v = _dot_f32_lhs(sc, exp_v_ref[...])
    sc_k = _dot_f32_lhs(sc, exp_k_ref[...])
    wg_v, a_v, g_v = sc_v[:q_len], sc_v[q_len:q_len + 1], sc_v[q_len + SUBLANES:q_len + SUBLANES + 1]
    wg_k, a_k, g_k = sc_k[:q_len], sc_k[q_len:q_len + 1], sc_k[q_len + SUBLANES:q_len + SUBLANES + 1]

    q = q_ref[0]
    k = k_ref[0]
    v = v_ref[0]
    q_b, k_b, v_b = q.astype(BF16), k.astype(BF16), v.astype(BF16)
    vw_b = (v * wg_v).astype(BF16)
    n_in = n_st[0:1, :]
    n_loc = jnp.sum(k * wg_k, axis=0, keepdims=True)
    d_t = (li - bcs).T

    outs, c_new = [], []
    for h in range(ML_HEADS):
        ks = slice(h * ML_QK_DIM, (h + 1) * ML_QK_DIM)
        vs = slice(h * ML_V_DIM, (h + 1) * ML_V_DIM)
        c_new.append(_dot(k[:, ks].T.astype(BF16), vw_b[:, vs]))
        dm = jnp.where(tril, bcs[:, h:h + 1] + d_t[h:h + 1, :], -jnp.inf)
        inter = bcs[:, h:h + 1] + m_in[:, h:h + 1]
        m_t = jnp.maximum(jnp.max(dm, axis=1, keepdims=True), inter)
        s = _dot_nt(q_b[:, ks], k_b[:, ks]) * jnp.exp(dm - m_t)
        w_int = jnp.exp(inter - m_t)
        num = _dot(s.astype(BF16), v_b[:, vs]) + w_int * _dot(q_b[:, ks], c_st[:, vs].astype(BF16))
        den = (jnp.sum(s, axis=1, keepdims=True)
               + w_int * jnp.sum(q[:, ks] * n_in[:, ks], axis=1, keepdims=True))
        outs.append(num / jnp.maximum(jnp.abs(den), jnp.exp(-m_t)))
    nrm = nrm_ref[...]
    hn = jnp.concatenate([_rms(outs[h], nrm[:, h * ML_V_DIM:(h + 1) * ML_V_DIM]) for h in range(ML_HEADS)],
                         axis=1)
    h_ref[0] = (hn * og_ref[0]).astype(h_ref.dtype)

    c_st[...] = a_v * c_st[...] + g_v * jnp.concatenate(c_new, axis=1)
    n_st[...] = jnp.broadcast_to(a_k * n_in + g_k * n_loc, n_st.shape)
    m_st[...] = jnp.broadcast_to(m_new, m_st.shape)


MIXER_SEQS_PER_STEP = 4


def _mixer_body(z_ref, xc_ref, dt_ref, q_ref, k_ref, v_ref, o_ref, i_ref, f_ref,
                alog_ref, dskip_ref, snrm_ref, mnrm_ref,
                tril_ref, exp_h_ref, exp_v_ref, exp_k_ref, y_ref, h_ref, state, c_st, n_st, m_st):
    @pl.when(pl.program_id(1) == 0)
    def _():
        for st_ref in (state, c_st, n_st, m_st):
            st_ref[...] = jnp.zeros(st_ref.shape, F32)

    for bi in range(z_ref.shape[0]):
        one = lambda ref: ref.at[bi:bi + 1]
        _ssd_body(one(z_ref), one(xc_ref), one(dt_ref), alog_ref, dskip_ref, snrm_ref, tril_ref, exp_h_ref,
                  one(y_ref), state.at[bi])
        _mlstm_body(one(q_ref), one(k_ref), one(v_ref), one(o_ref), one(i_ref), one(f_ref), mnrm_ref,
                    tril_ref, exp_v_ref, exp_k_ref, one(h_ref), c_st.at[bi], n_st.at[bi], m_st.at[bi])


def _mixer(acts, params):
    b, l, _ = acts[0].shape
    kw = ML_HEADS * ML_QK_DIM
    consts = [_rhs_stack(np.tril(np.ones((CHUNK, CHUNK)))),
              _lhs_stack(_expander(LANES, SSD_WIDTH, SSD_HEAD_DIM)), _lhs_stack(_expander(LANES, ML_WIDTH, ML_V_DIM)),
              _lhs_stack(_expander(LANES, kw, ML_QK_DIM))]
    nb = MIXER_SEQS_PER_STEP if b % MIXER_SEQS_PER_STEP == 0 else 1
    blk = lambda w: pl.BlockSpec((nb, CHUNK, w), lambda i, c: (i, c, 0))
    return pl.pallas_call(
        _mixer_body,
        out_shape=[jax.ShapeDtypeStruct((b, l, SSD_WIDTH), BF16), jax.ShapeDtypeStruct((b, l, ML_WIDTH), BF16)],
        grid=(b // nb, l // CHUNK),
        in_specs=([blk(a.shape[2]) for a in acts] + [_resident(p.shape) for p in params]
                  + [_resident(c.shape) for c in consts]),
        out_specs=[blk(SSD_WIDTH), blk(ML_WIDTH)],
        scratch_shapes=[pltpu.VMEM((nb, SSD_STATE, SSD_WIDTH), F32),
                        pltpu.VMEM((nb, ML_QK_DIM, ML_WIDTH), F32),
                        pltpu.VMEM((nb, SUBLANES, kw), F32),
                        pltpu.VMEM((nb, SUBLANES, LANES), F32)],
        compiler_params=_params("parallel", "arbitrary"),
        name="mixer",
    )(*acts, *params, *consts)


def _rope(x, cos, sin, first_half):
    partner = jnp.where(first_half, pltpu.roll(x, LANES - ATT_HEAD_DIM // 2, 1), pltpu.roll(x, ATT_HEAD_DIM // 2, 1))
    return x * cos + partner * sin


def _sa_proj_body(x_ref, g_ref, w_ref, gq_ref, gk_ref, cs_ref, seg_ref, expand_ref,
                  qe_ref, qo_ref, qie_ref, qio_ref, ka_ref, kb_ref, vt_ref, wt_ref):
    tm = x_ref.shape[0]
    tk = vt_ref.shape[2]
    nq = ATT_HEADS * ATT_HEAD_DIM
    nqi = IDX_HEADS * IDX_DIM
    h = _rms(x_ref[...], g_ref[...]).astype(BF16)
    proj = _dot(h, w_ref[...])
    lane = _iota((tm, LANES), 1)
    half = ATT_HEAD_DIM // 2
    first_half = (lane % ATT_HEAD_DIM) < half
    low_head = lane < ATT_HEAD_DIM
    cs = cs_ref[...]
    cos = jnp.where(lane < half, cs, 0.0)
    cos = cos + pltpu.roll(cos, half, 1)
    cos = cos + pltpu.roll(cos, ATT_HEAD_DIM, 1)
    sin = jnp.where((lane >= half) & low_head, cs, 0.0)
    sin = sin - pltpu.roll(sin, LANES - half, 1)
    sin = sin + pltpu.roll(sin, ATT_HEAD_DIM, 1)

    q = proj[:, :nq]
    ms = _dot_f32_lhs(q * q, seg_ref[...]) * (1.0 / ATT_HEAD_DIM)
    rs = _dot_f32_lhs(lax.rsqrt(ms + EPS), expand_ref[...])
    qn = q * rs * gq_ref[...]
    q_scale = ATT_HEAD_DIM ** -0.5 * LOG2_E

    def emit(even_ref, odd_ref, slab, s0):
        zeros = jnp.zeros((ATT_HEAD_DIM, CHUNK), F32)
        for b0 in range(0, tm, CHUNK):
            feat = slab[b0:b0 + CHUNK].T
            even_ref[b0 // CHUNK, :, s0:s0 + LANES] = jnp.concatenate([feat[:ATT_HEAD_DIM], zeros], 0).astype(BF16)
            odd_ref[b0 // CHUNK, :, s0:s0 + LANES] = jnp.concatenate([zeros, feat[ATT_HEAD_DIM:]], 0).astype(BF16)

    for s0 in range(0, nq, LANES):
        emit(qe_ref, qo_ref, _rope(qn[:, s0:s0 + LANES], cos, sin, first_half) * q_scale, s0)
    for s0 in range(0, nqi, LANES):
        emit(qie_ref, qio_ref, _rope(proj[:, nq + s0:nq + s0 + LANES], cos, sin, first_half), s0)

    kk = proj[:, nq + nqi:nq + nqi + LANES]
    k_ms = jnp.sum(jnp.where(low_head, kk * kk, 0.0), axis=1, keepdims=True) * (1.0 / ATT_HEAD_DIM)
    kk = jnp.where(low_head, kk * lax.rsqrt(k_ms + EPS) * gk_ref[...], kk)
    kk = _rope(kk, cos, sin, first_half)
    ka_ref[...] = kk.astype(BF16)
    kb_ref[...] = pltpu.roll(kk, ATT_HEAD_DIM, 1).astype(BF16)
    vv = proj[:, nq + nqi + LANES:nq + nqi + 2 * LANES]
    for c in range(tm // tk):
        vt_ref[c] = vv[c * tk:(c + 1) * tk, :].T[:ATT_HEAD_DIM, :].astype(BF16)
    wt_ref[...] = proj[:, nq + nqi + 2 * LANES:].T[:SUBLANES, :]


def _sa_proj(x2d, g, w, gq, gk, rope, *, tm, tk):
    t, d = x2d.shape
    nq = ATT_HEADS * ATT_HEAD_DIM
    nqi = IDX_HEADS * IDX_DIM
    row = lambda wd: pl.BlockSpec((tm, wd), lambda i: (i, 0))
    qblocks = lambda wd: jax.ShapeDtypeStruct((t // CHUNK, LANES, wd), BF16)
    qspec = lambda wd: pl.BlockSpec((tm // CHUNK, LANES, wd), lambda i: (i, 0, 0))
    outs = [qblocks(nq), qblocks(nq), qblocks(nqi), qblocks(nqi),
            jax.ShapeDtypeStruct((t, LANES), BF16), jax.ShapeDtypeStruct((t, LANES), BF16),
            jax.ShapeDtypeStruct((t // tk, ATT_HEAD_DIM, tk), BF16),
            jax.ShapeDtypeStruct((SUBLANES, t), F32)]
    head_of_lane = _expander(LANES, nq, ATT_HEAD_DIM)
    expand, seg = _lhs_stack(head_of_lane), _lhs_stack(head_of_lane.T)
    return pl.pallas_call(
        _sa_proj_body,
        out_shape=outs,
        grid=(t // tm,),
        in_specs=[row(d), _resident(g.shape), _resident(w.shape), _resident(gq.shape), _resident(gk.shape),
                  row(LANES), _resident(seg.shape), _resident(expand.shape)],
        out_specs=[qspec(nq), qspec(nq), qspec(nqi), qspec(nqi), row(LANES), row(LANES),
                   pl.BlockSpec((tm // tk, ATT_HEAD_DIM, tk), lambda i: (i, 0, 0)),
                   pl.BlockSpec((SUBLANES, tm), lambda i: (0, i))],
        compiler_params=_params("parallel"),
        name="sa_proj",
    )(x2d, g, w, gq, gk, rope, seg, expand)


INT_MIN = -2 ** 31
NEG_INF_KEY = INT_MIN + 0x7FFFFF
ATT_KEY_CHUNK = 256


def _sortable_key(x):
    bits = pltpu.bitcast(jnp.where(x == 0.0, 0.0, x), I32)
    return bits ^ ((bits >> 31) & 0x7FFFFFFF)


def _dsa_body(qe_ref, qo_ref, qie_ref, qio_ref, wt_ref, ka_ref, kb_ref, vt_ref, o_ref,
              skey, last_s, m_s, l_s, a_s, acc_s, p_s, *, topk, seq_bits):
    blk = CHUNK
    tk = vt_ref.shape[2]
    j = pl.program_id(1)
    n_chunks = lax.div(j * blk + blk + tk - 1, tk)
    row = _iota((tk, LANES), 0)
    q_pos = j * blk + _iota((tk, LANES), 1)

    qe, qo, qie, qio = qe_ref.at[0], qo_ref.at[0], qie_ref.at[0], qio_ref.at[0]
    n_pairs = ATT_HEADS // 2
    n_ipairs = IDX_HEADS // 2

    idx_scale = (IDX_HEADS ** -0.5) * (IDX_DIM ** -0.5)
    wt = wt_ref[...]

    def chunk(ci):
        return pl.ds(pl.multiple_of(ci * tk, tk), tk)

    cw = 2 * LANES

    def score_chunk(ci):
        sc = jnp.zeros((tk, LANES), F32)
        for par, (src, k_ref) in enumerate(((qie, kb_ref), (qio, ka_ref))):
            kc = k_ref[0, chunk(ci), :]
            for c0 in range(0, n_ipairs * blk, cw):
                logits = _dot(kc, src[:, c0:c0 + cw])
                for u in range(cw // blk):
                    head = 2 * (c0 // blk + u) + par
                    sc = sc + jnp.maximum(logits[:, u * blk:(u + 1) * blk], 0.0) * wt[head:head + 1, :]
        sc = sc * idx_scale
        sc = jnp.where(ci * tk + row <= q_pos, sc, -jnp.inf)
        skey[chunk(ci), :] = _sortable_key(sc)

    def score_chunk_pair(i, _):
        score_chunk(2 * i)
        score_chunk(2 * i + 1)
        return 0

    lax.fori_loop(0, lax.shift_right_logical(n_chunks, 1), score_chunk_pair, 0)

    @pl.when((n_chunks & 1) == 1)
    def _():
        score_chunk(n_chunks - 1)

    acc_rows = 4 * SUBLANES

    def count(pred_fn):
        def body(ci, c):
            hit = jnp.where(pred_fn(skey[chunk(ci), :], ci), 1.0, 0.0)
            return c + jnp.sum(hit.reshape(tk // acc_rows, acc_rows, LANES), axis=0)
        part = lax.fori_loop(0, n_chunks, body, jnp.zeros((acc_rows, LANES), F32))
        return jnp.sum(part, axis=0, keepdims=True)

    kf = float(topk)
    thr = jnp.full((1, LANES), INT_MIN, I32)
    n_ge = jnp.full((1, LANES), 1.0, F32) * (n_chunks * tk).astype(F32)
    for bit in range(31, -1, -1):
        cand = thr + jnp.int32(INT_MIN if bit == 31 else 1 << bit)
        c = count(lambda key, ci, cand=cand: key >= cand)
        thr = jnp.where(c >= kf, cand, thr)
        n_ge = jnp.where(c >= kf, c, n_ge)

    last_s[...] = jnp.full(last_s.shape, 2 ** 31 - 1, I32)
    tied = (n_ge > kf) & (thr > NEG_INF_KEY)

    @pl.when(jnp.sum(jnp.where(tied, 1.0, 0.0)) > 0.0)
    def _():
        need = kf - count(lambda key, ci: key > thr)
        last = jnp.zeros((1, LANES), I32)
        for bit in range(seq_bits - 1, -1, -1):
            cand = last + jnp.int32(1 << bit)
            c = count(lambda key, ci, cand=cand: (key == thr) & (ci * tk + row < cand))
            last = jnp.where(c <= need - 1.0, cand, last)
        last_s[...] = jnp.broadcast_to(last, last_s.shape)

    last = last_s[0:1, :]

    m_s[...] = jnp.full(m_s.shape, -1e30, F32)
    l_s[...] = jnp.zeros(l_s.shape, F32)
    acc_s[...] = jnp.zeros(acc_s.shape, F32)
    half = n_pairs * blk

    def masked_logits(ci):
        key = skey[chunk(ci), :]
        pos = ci * tk + row
        sel = ((key > thr) | ((key == thr) & (pos <= last))) & (pos <= q_pos)
        bias = jnp.where(sel, 0.0, -jnp.inf)
        bias = jnp.concatenate([bias] * (cw // blk), axis=1)
        for src, k_ref, base in ((qe, ka_ref, 0), (qo, kb_ref, half)):
            kc = k_ref[0, chunk(ci), :]
            for c0 in range(0, half, cw):
                yield base + c0, _dot(kc, src[:, c0:c0 + cw]) + bias

    def probabilities(ci):
        m_chunk = [jnp.max(s, axis=0, keepdims=True) for _, s in masked_logits(ci)]
        m_prev = m_s[...]
        m_cur = jnp.maximum(m_prev, jnp.concatenate(m_chunk, axis=1))
        alpha = jnp.exp2(m_prev - m_cur)
        l_chunk = []
        for c0, s in masked_logits(ci):
            p = jnp.exp2(s - m_cur[:, c0:c0 + cw])
            l_chunk.append(jnp.sum(p, axis=0, keepdims=True))
            p_s[:, c0:c0 + cw] = p.astype(BF16)
        l_s[...] = alpha * l_s[...] + jnp.concatenate(l_chunk, axis=1)
        m_s[...] = m_cur
        a_s[...] = alpha

    def accumulate(ci):
        acc_s[...] = a_s[...] * acc_s[...] + _dot(vt_ref[ci], p_s[...])

    def attn_chunk(ci):
        accumulate(ci - 1)
        probabilities(ci)

    def attn_chunk_pair(i, _):
        attn_chunk(2 * i + 1)
        attn_chunk(2 * i + 2)
        return 0

    probabilities(jnp.int32(0))
    n_rest = n_chunks - 1
    lax.fori_loop(0, lax.shift_right_logical(n_rest, 1), attn_chunk_pair, 0)

    @pl.when((n_rest & 1) == 1)
    def _():
        attn_chunk(n_rest)

    accumulate(n_rest)

    out = acc_s[...] / l_s[...]
    for p in range(n_pairs):
        pair = jnp.concatenate([out[:, p * blk:(p + 1) * blk], out[:, half + p * blk:half + (p + 1) * blk]], axis=0)
        o_ref[0, :, p * LANES:(p + 1) * LANES] = pair.T.astype(o_ref.dtype)


def _dsa(qe, qo, qie, qio, wt, ka, kb, vt, *, topk):
    b, l, _ = ka.shape
    nb = l // CHUNK
    tk = vt.shape[2]
    rows = (ATT_HEADS // 2) * CHUNK
    nq = ATT_HEADS * ATT_HEAD_DIM
    qop = lambda a: pl.BlockSpec((1,) + a.shape[1:], lambda i, j: (i * nb + j, 0, 0))
    seq = pl.BlockSpec((1, l, LANES), lambda i, j: (i, 0, 0))
    return pl.pallas_call(
        functools.partial(_dsa_body, topk=topk, seq_bits=int(np.log2(l))),
        out_shape=jax.ShapeDtypeStruct((b, l, nq), BF16),
        grid=(b, nb),
        in_specs=[qop(qe), qop(qo), qop(qie), qop(qio),
                  pl.BlockSpec((SUBLANES, CHUNK), lambda i, j: (0, i * nb + j)),
                  seq, seq, pl.BlockSpec((l // tk, ATT_HEAD_DIM, tk), lambda i, j: (i, 0, 0))],
        out_specs=pl.BlockSpec((1, CHUNK, nq), lambda i, j: (i, j, 0)),
        scratch_shapes=[pltpu.VMEM((l, LANES), I32), pltpu.VMEM((SUBLANES, LANES), I32),
                        pltpu.VMEM((1, 2 * rows), F32), pltpu.VMEM((1, 2 * rows), F32),
                        pltpu.VMEM((1, 2 * rows), F32), pltpu.VMEM((ATT_HEAD_DIM, 2 * rows), F32),
                        pltpu.VMEM((tk, 2 * rows), BF16)],
        compiler_params=_params("parallel", "arbitrary"),
        name="dsa",
    )(qe, qo, qie, qio, wt, ka, kb, vt)


def _split_cols(w, sizes):
    out, c0 = [], 0
    for s in sizes:
        out.append(w[:, c0:c0 + s])
        c0 += s
    return out


def _pad_cols(w, width):
    return jnp.pad(w, ((0, 0), (0, width - w.shape[1])))


def _pad_row(v, width):
    return jnp.pad(v, (0, width - v.shape[0]))[None, :]


def _hybrid_mixer_parts(x2d, b, l, mix_norm, w_in, conv_w, conv_b, dt_bias, a_log, d_skip, ssd_norm,
                        igate_bias, fgate_bias, mlstm_norm, *, tm):
    wz, wxbc, wdt, wq, wk, wv, wo, wi, wf = _split_cols(w_in, HY_SPLITS)
    w = jnp.concatenate([wz, wxbc, wq, wk, wv, wo, _pad_cols(wdt, LANES), _pad_cols(wi, LANES),
                         _pad_cols(wf, LANES)], axis=1).astype(BF16)
    z, xc, qm, km, vm, om, dts, ig, fg = _hy_proj(
        x2d, mix_norm[None, :], w, conv_w, conv_b[None, :], _pad_row(dt_bias, LANES), _pad_row(igate_bias, LANES),
        _pad_row(fgate_bias, LANES), l, tm=tm)
    acts = [a.reshape(b, l, a.shape[1]) for a in (z, xc, dts, qm, km, vm, om, ig, fg)]
    params = [_pad_row(a_log, LANES), jnp.repeat(d_skip, SSD_HEAD_DIM)[None, :], ssd_norm[None, :],
              mlstm_norm[None, :]]
    y, hm = _mixer(acts, params)
    return y.reshape(b * l, SSD_WIDTH), hm.reshape(b * l, ML_WIDTH)


def _rope_table(positions):
    inv = ROPE_THETA ** (-jnp.arange(0, ATT_HEAD_DIM, 2, dtype=F32) / ATT_HEAD_DIM)
    ang = positions.astype(F32).reshape(-1, 1) * inv
    return jnp.concatenate([jnp.cos(ang), jnp.sin(ang), jnp.zeros_like(ang), jnp.zeros_like(ang)], axis=1)


def _sparse_attention_parts(x2d, b, l, rope, mix_norm, w_in, q_norm, k_norm, topk, *, tm):
    wq, wk, wv, wqi, wki, wwi = _split_cols(w_in, SA_SPLITS)
    w = jnp.concatenate([wq, wqi, wk, wki, _pad_cols(wv, LANES), _pad_cols(wwi, LANES)], axis=1).astype(BF16)
    gq = jnp.tile(q_norm, ATT_HEADS)[None, :]
    gk = _pad_row(k_norm, LANES)
    qe, qo, qie, qio, ka, kb, vt, wt = _sa_proj(x2d, mix_norm[None, :], w, gq, gk, rope, tm=tm, tk=ATT_KEY_CHUNK)
    r3 = lambda a: a.reshape(b, l, a.shape[1])
    o = _dsa(qe, qo, qie, qio, wt, r3(ka), r3(kb), vt, topk=topk)
    return o.reshape(b * l, ATT_HEADS * ATT_HEAD_DIM)


def kernel(x, positions, ffn_norm, ffn_w_gate, ffn_w_up, ffn_w_down, mix_norm, hy_w_in, hy_conv_w, hy_conv_b,
           hy_dt_bias, hy_a_log, hy_d_skip, hy_ssd_norm, hy_igate_bias, hy_fgate_bias, hy_mlstm_norm, hy_w_out,
           sa_w_in, sa_q_norm, sa_k_norm, sa_w_out):
    b, l, d = x.shape
    depth = ffn_norm.shape[0]
    t = b * l
    tm = 512 if t % 512 == 0 else CHUNK
    topk = min(TOPK_MAX, l // 4)
    rope = _rope_table(positions)
    wg, wu, wd = ffn_w_gate, ffn_w_up, ffn_w_down
    x2d = x.reshape(t, d)
    for layer in range(depth):
        x2d = _ffn(x2d, ffn_norm[layer, 0][None, :], wg, wu, wd, layer, 0, tm=tm)
        if layer % 2 == 0:
            e = layer // 2
            y, hm = _hybrid_mixer_parts(x2d, b, l, mix_norm[layer], hy_w_in[e], hy_conv_w[e], hy_conv_b[e],
                                        hy_dt_bias[e], hy_a_log[e], hy_d_skip[e], hy_ssd_norm[e],
                                        hy_igate_bias[e], hy_fgate_bias[e], hy_mlstm_norm[e], tm=tm)
            w_out = hy_w_out[e].astype(BF16)
            acts, ws = [y, hm], [w_out[:SSD_WIDTH], w_out[SSD_WIDTH:]]
        else:
            o = layer // 2
            att = _sparse_attention_parts(x2d, b, l, rope, mix_norm[layer], sa_w_in[o], sa_q_norm[o],
                                          sa_k_norm[o], topk, tm=tm)
            acts, ws = [att], [sa_w_out[o].astype(BF16)]
        x2d = _proj_ffn(x2d, acts, ws, ffn_norm[layer, 1][None, :], wg, wu, wd, layer, 1, tm=tm)
    return x2d.reshape(b, l, d)
```

```python
import functools

import jax
import jax.numpy as jnp
import numpy as np
from jax import lax
from jax.experimental import pallas as pl
from jax.experimental.pallas import tpu as pltpu

F32 = jnp.float32
BF16 = jnp.bfloat16
I32 = jnp.int32

EPS = 1e-6
ROPE_THETA = 10000.0
LOG2_E = 1.4426950408889634
LANES = 128
SUBLANES = 8
CHUNK = 128
CONV_K = 4
TOPK_MAX = 256

D_MODEL = 1024
D_FF = 2816
SSD_WIDTH = 1024
SSD_HEAD_DIM = 64
SSD_HEADS = 16
SSD_GROUPS = 2
SSD_STATE = 128
SSD_CONV_DIM = SSD_WIDTH + 2 * SSD_GROUPS * SSD_STATE
ML_WIDTH = 1024
ML_HEADS = 4
ML_V_DIM = 256
ML_QK_DIM = 128
HY_SPLITS = (SSD_WIDTH, SSD_CONV_DIM, SSD_HEADS, ML_HEADS * ML_QK_DIM, ML_HEADS * ML_QK_DIM,
             ML_WIDTH, ML_WIDTH, ML_HEADS, ML_HEADS)
ATT_HEAD_DIM = 64
ATT_HEADS = 16
IDX_HEADS = 8
IDX_DIM = 64
SA_SPLITS = (ATT_HEADS * ATT_HEAD_DIM, ATT_HEAD_DIM, ATT_HEAD_DIM, IDX_HEADS * IDX_DIM, IDX_DIM, IDX_HEADS)

VMEM_LIMIT_BYTES = 56 * 1024 * 1024

NT_DIMS = (((1,), (1,)), ((), ()))


def _params(*sem):
    return pltpu.CompilerParams(dimension_semantics=sem, vmem_limit_bytes=VMEM_LIMIT_BYTES)


def _resident(shape):
    nd = len(shape)
    return pl.BlockSpec(shape, lambda *_: (0,) * nd, pipeline_mode=pl.Buffered(1))


def _dot(a, b):
    return jnp.dot(a, b, preferred_element_type=F32)


def _dot_nt(a, b):
    return lax.dot_general(a, b, NT_DIMS, preferred_element_type=F32)


def _split3(a):
    a1 = a.astype(BF16)
    r = a - a1.astype(F32)
    a2 = r.astype(BF16)
    r = r - a2.astype(F32)
    return a1, a2, r.astype(BF16)


N_SPLIT = 3


def _dot_f32_lhs(a, b01_stack):
    return _dot(jnp.concatenate(_split3(a), axis=1), b01_stack)


def _dot_f32_rhs(a01_stack, b):
    return _dot(a01_stack, jnp.concatenate(_split3(b), axis=0))


def _rms(x, g):
    return x * lax.rsqrt(jnp.mean(x * x, axis=-1, keepdims=True) + EPS) * g


def _sigmoid(x):
    return 1.0 / (1.0 + jnp.exp(-x))


def _softplus(x):
    return jnp.maximum(x, 0.0) + jnp.log1p(jnp.exp(-jnp.abs(x)))


def _iota(shape, dim):
    return lax.broadcasted_iota(I32, shape, dim)


def _expander(rows, cols, width):
    return np.arange(cols)[None, :] // width == np.arange(rows)[:, None]


def _lhs_stack(b01):
    return jnp.asarray(np.tile(b01, (N_SPLIT, 1)), dtype=BF16)


def _rhs_stack(a01):
    return jnp.asarray(np.tile(a01, (1, N_SPLIT)), dtype=BF16)


def _stacked(w, layer, idx):
    return pl.BlockSpec((None, None) + w.shape[2:], lambda *_: (layer, idx, 0, 0), pipeline_mode=pl.Buffered(1))


def _tril(n):
    return (_iota((n, n), 0) >= _iota((n, n), 1))


def _swiglu_half(x, g_ref, wg_ref, wu_ref, wd_ref, ff_chunk):
    h = _rms(x, g_ref[...]).astype(BF16)
    acc = jnp.zeros(x.shape, F32)
    for c0 in range(0, wg_ref.shape[1], ff_chunk):
        gate = _dot(h, wg_ref[:, c0:c0 + ff_chunk].astype(BF16))
        up = _dot(h, wu_ref[:, c0:c0 + ff_chunk].astype(BF16))
        act = (gate * _sigmoid(gate) * up).astype(BF16)
        acc = acc + _dot(act, wd_ref[c0:c0 + ff_chunk, :].astype(BF16))
    return x + 0.5 * acc


def _ffn_body(x_ref, g_ref, wg_ref, wu_ref, wd_ref, o_ref, *, ff_chunk):
    o_ref[...] = _swiglu_half(x_ref[...], g_ref, wg_ref, wu_ref, wd_ref, ff_chunk)


def _ffn(x2d, g, wg, wu, wd, layer, idx, *, tm, ff_chunk=256):
    t, d = x2d.shape
    return pl.pallas_call(
        functools.partial(_ffn_body, ff_chunk=ff_chunk),
        out_shape=jax.ShapeDtypeStruct((t, d), F32),
        grid=(t // tm,),
        in_specs=[pl.BlockSpec((tm, d), lambda i: (i, 0)), _resident(g.shape),
                  _stacked(wg, layer, idx), _stacked(wu, layer, idx), _stacked(wd, layer, idx)],
        out_specs=pl.BlockSpec((tm, d), lambda i: (i, 0)),
        compiler_params=_params("parallel"),
        name="ffn",
    )(x2d, g, wg, wu, wd)


def _proj_ffn_body(*refs, n_in, ff_chunk):
    x_ref = refs[0]
    a_refs = refs[1:1 + n_in]
    w_refs = refs[1 + n_in:1 + 2 * n_in]
    g_ref, wg_ref, wu_ref, wd_ref, o_ref = refs[1 + 2 * n_in:]
    x = x_ref[...]
    for a_ref, w_ref in zip(a_refs, w_refs):
        x = x + _dot(a_ref[...], w_ref[...])
    o_ref[...] = _swiglu_half(x, g_ref, wg_ref, wu_ref, wd_ref, ff_chunk)


def _proj_ffn(x2d, acts, ws, g, wg, wu, wd, layer, idx, *, tm, ff_chunk=256):
    t, d = x2d.shape
    n_in = len(acts)
    return pl.pallas_call(
        functools.partial(_proj_ffn_body, n_in=n_in, ff_chunk=ff_chunk),
        out_shape=jax.ShapeDtypeStruct((t, d), F32),
        grid=(t // tm,),
        in_specs=([pl.BlockSpec((tm, d), lambda i: (i, 0))]
                  + [pl.BlockSpec((tm, a.shape[1]), lambda i: (i, 0)) for a in acts]
                  + [_resident(w.shape) for w in ws]
                  + [_resident(g.shape), _stacked(wg, layer, idx), _stacked(wu, layer, idx),
                     _stacked(wd, layer, idx)]),
        out_specs=pl.BlockSpec((tm, d), lambda i: (i, 0)),
        compiler_params=_params("parallel"),
        name="proj_ffn",
    )(x2d, *acts, *ws, g, wg, wu, wd)


def _hy_proj_body(x_ref, g_ref, w_ref, cw_ref, cb_ref, dtb_ref, ib_ref, fb_ref, z_ref, xc_ref, q_ref, k_ref, v_ref,
                  o_ref, dt_ref, i_ref, f_ref, tails, *, tiles_per_seq):
    tm = x_ref.shape[0]
    lane = _iota((tm, LANES), 1)
    ssd_head, ml_head = lane < SSD_HEADS, lane < ML_HEADS
    same = lambda a: a
    outputs = (
        (z_ref, lambda a: a * _sigmoid(a)),
        (None, None),
        (q_ref, lambda a: a * (ML_QK_DIM ** -0.5)),
        (k_ref, same),
        (v_ref, same),
        (o_ref, _sigmoid),
        (dt_ref, lambda a: jnp.where(ssd_head, _softplus(a + dtb_ref[...]), 0.0)),
        (i_ref, lambda a: jnp.where(ml_head, a + ib_ref[...], 0.0)),
        (f_ref, lambda a: jnp.where(ml_head, -_softplus(-(a + fb_ref[...])), 0.0)),
    )

    @pl.when(pl.program_id(0) % tiles_per_seq == 0)
    def _():
        tails[...] = jnp.zeros(tails.shape, F32)

    h = _rms(x_ref[...], g_ref[...]).astype(BF16)

    u = _dot(h, w_ref[:, SSD_WIDTH:SSD_WIDTH + SSD_CONV_DIM])
    first_row = _iota((SUBLANES, SSD_CONV_DIM), 0) == 0
    part = cw_ref[0:1, :] * u
    for k in range(1, CONV_K):
        rolled = pltpu.roll(part, 1, 0)
        head = jnp.where(first_row, pltpu.roll(tails[k - 1], 1, 0), rolled[:SUBLANES])
        tails[k - 1] = part[tm - SUBLANES:, :]
        part = jnp.concatenate([head, rolled[SUBLANES:]], axis=0) + cw_ref[k:k + 1, :] * u
    conv = part + cb_ref[...]
    xc_ref[...] = (conv * _sigmoid(conv)).astype(xc_ref.dtype)

    c0 = 0
    for out_ref, finish in outputs:
        if out_ref is None:
            c0 += SSD_CONV_DIM
            continue
        wd = out_ref.shape[1]
        out_ref[...] = finish(_dot(h, w_ref[:, c0:c0 + wd])).astype(out_ref.dtype)
        c0 += wd


def _hy_proj(x2d, g, w, conv_w, conv_b, dt_bias, ib, fb, seq_len, *, tm):
    t, d = x2d.shape
    kw = ML_HEADS * ML_QK_DIM
    widths = (SSD_WIDTH, SSD_CONV_DIM, kw, kw, ML_WIDTH, ML_WIDTH, LANES, LANES, LANES)
    return pl.pallas_call(
        functools.partial(_hy_proj_body, tiles_per_seq=seq_len // tm),
        out_shape=[jax.ShapeDtypeStruct((t, wd), F32) for wd in widths],
        grid=(t // tm,),
        in_specs=[pl.BlockSpec((tm, d), lambda i: (i, 0)), _resident(g.shape), _resident(w.shape),
                  _resident(conv_w.shape), _resident(conv_b.shape), _resident(dt_bias.shape),
                  _resident(ib.shape), _resident(fb.shape)],
        out_specs=[pl.BlockSpec((tm, wd), lambda i: (i, 0)) for wd in widths],
        scratch_shapes=[pltpu.VMEM((CONV_K - 1, SUBLANES, SSD_CONV_DIM), F32)],
        compiler_params=_params("arbitrary"),
        name="hy_proj",
    )(x2d, g, w, conv_w, conv_b, dt_bias, ib, fb)


def _ssd_body(zg_ref, xc_ref, dt_ref, alog_ref, dskip_ref, nrm_ref, tril_ref, expand_ref, y_ref, state):
    q_len = CHUNK
    hp = SSD_WIDTH // SSD_GROUPS
    xc = xc_ref[0]
    xs = xc[:, :SSD_WIDTH]
    bm = xc[:, SSD_WIDTH:SSD_WIDTH + SSD_GROUPS * SSD_STATE]
    cm = xc[:, SSD_WIDTH + SSD_GROUPS * SSD_STATE:]

    dt = dt_ref[0]
    a = dt * (-jnp.exp(alog_ref[...]))
    tril = _tril(q_len)
    acs = _dot_f32_rhs(tril_ref[...], a)
    eacs = jnp.exp(acs)
    dend = jnp.exp(acs[q_len - 1:q_len, :] - acs)
    wide = _dot_f32_lhs(jnp.concatenate([dt, eacs, dend], axis=0), expand_ref[...])
    dt_x, eacs_x, dend_x = wide[:q_len], wide[q_len:2 * q_len], wide[2 * q_len:]

    xd = xs * dt_x
    lane_w = _iota((q_len, SSD_WIDTH), 1)
    low_half = (lane_w % LANES) < SSD_HEAD_DIM
    xd_halves = (jnp.where(low_half, xd, 0.0).astype(BF16), jnp.where(low_half, 0.0, xd).astype(BF16))
    xdd_b = (xd * dend_x).astype(BF16)
    acs_t = acs.T
    st = state[...]
    st_b = st.astype(BF16)

    y_diag, y_off, s_new = [], [], []
    heads_per_group = SSD_HEADS // SSD_GROUPS
    for g in range(SSD_GROUPS):
        bg = bm[:, g * SSD_STATE:(g + 1) * SSD_STATE]
        cg_b = cm[:, g * SSD_STATE:(g + 1) * SSD_STATE].astype(BF16)
        cb = _dot_nt(cg_b, bg.astype(BF16))
        s_new.append(_dot(bg.T.astype(BF16), xdd_b[:, g * hp:(g + 1) * hp]))
        y_off.append(_dot(cg_b, st_b[:, g * hp:(g + 1) * hp]))
        for pair in range(heads_per_group // 2):
            acc = None
            for par in range(2):
                h = g * heads_per_group + 2 * pair + par
                seg = acs[:, h:h + 1] - acs_t[h:h + 1, :]
                m = (cb * jnp.exp(jnp.where(tril, seg, -jnp.inf))).astype(BF16)
                slab = (h // 2) * LANES
                d = _dot(m, xd_halves[par][:, slab:slab + LANES])
                acc = d if acc is None else acc + d
            y_diag.append(acc)
    y = (jnp.concatenate(y_diag, axis=1) + eacs_x * jnp.concatenate(y_off, axis=1) + xs * dskip_ref[...])
    state[...] = st * eacs_x[q_len - 1:q_len, :] + jnp.concatenate(s_new, axis=1)

    yg = y * zg_ref[0]
    nrm = nrm_ref[...]
    y_ref[0] = jnp.concatenate(
        [_rms(yg[:, g * hp:(g + 1) * hp], nrm[:, g * hp:(g + 1) * hp]) for g in range(SSD_GROUPS)],
        axis=1).astype(y_ref.dtype)


def _mlstm_body(q_ref, k_ref, v_ref, og_ref, i_ref, f_ref, nrm_ref, tril_ref, exp_v_ref, exp_k_ref,
                h_ref, c_st, n_st, m_st):
    q_len = CHUNK

    li = i_ref[0]
    lf = f_ref[0]
    tril = _tril(q_len)
    bcs = _dot_f32_rhs(tril_ref[...], lf)
    b_last = bcs[q_len - 1:q_len, :]
    gl = b_last - bcs + li
    m_loc = jnp.max(gl, axis=0, keepdims=True)
    wg = jnp.exp(gl - m_loc)
    m_in = m_st[0:1, :]
    m_new = jnp.maximum(b_last + m_in, m_loc)
    a_sc = jnp.exp(b_last + m_in - m_new)
    g_sc = jnp.exp(m_loc - m_new)

    sc = jnp.concatenate([wg, jnp.broadcast_to(a_sc, (SUBLANES, LANES)),
                          jnp.broadcast_to(g_sc, (SUBLANES, LANES))], axis=0)
    sc_v = _dot_f32_lhs(sc, exp_v_ref[...])
    sc_k = _dot_f32_lhs(sc, exp_k_ref[...])
    wg_v, a_v, g_v = sc_v[:q_len], sc_v[q_len:q_len + 1], sc_v[q_len + SUBLANES:q_len + SUBLANES + 1]
    wg_k, a_k, g_k = sc_k[:q_len], sc_k[q_len:q_len + 1], sc_k[q_len + SUBLANES:q_len + SUBLANES + 1]

    q = q_ref[0]
    k = k_ref[0]
    v = v_ref[0]
    q_b, k_b, v_b = q.astype(BF16), k.astype(BF16), v.astype(BF16)
    vw_b = (v * wg_v).astype(BF16)
    n_in = n_st[0:1, :]
    n_loc = jnp.sum(k * wg_k, axis=0, keepdims=True)
    d_t = (li - bcs).T

    outs, c_new = [], []
    for h in range(ML_HEADS):
        ks = slice(h * ML_QK_DIM, (h + 1) * ML_QK_DIM)
        vs = slice(h * ML_V_DIM, (h + 1) * ML_V_DIM)
        c_new.append(_dot(k[:, ks].T.astype(BF16), vw_b[:, vs]))
        dm = jnp.where(tril, bcs[:, h:h + 1] + d_t[h:h + 1, :], -jnp.inf)
        inter = bcs[:, h:h + 1] + m_in[:, h:h + 1]
        m_t = jnp.maximum(jnp.max(dm, axis=1, keepdims=True), inter)
        s = _dot_nt(q_b[:, ks], k_b[:, ks]) * jnp.exp(dm - m_t)
        w_int = jnp.exp(inter - m_t)
        num = _dot(s.astype(BF16), v_b[:, vs]) + w_int * _dot(q_b[:, ks], c_st[:, vs].astype(BF16))
        den = (jnp.sum(s, axis=1, keepdims=True)
               + w_int * jnp.sum(q[:, ks] * n_in[:, ks], axis=1, keepdims=True))
        outs.append(num / jnp.maximum(jnp.abs(den), jnp.exp(-m_t)))
    nrm = nrm_ref[...]
    hn = jnp.concatenate([_rms(outs[h], nrm[:, h * ML_V_DIM:(h + 1) * ML_V_DIM]) for h in range(ML_HEADS)],
                         axis=1)
    h_ref[0] = (hn * og_ref[0]).astype(h_ref.dtype)

    c_st[...] = a_v * c_st[...] + g_v * jnp.concatenate(c_new, axis=1)
    n_st[...] = jnp.broadcast_to(a_k * n_in + g_k * n_loc, n_st.shape)
    m_st[...] = jnp.broadcast_to(m_new, m_st.shape)


MIXER_SEQS_PER_STEP = 4


def _mixer_body(z_ref, xc_ref, dt_ref, q_ref, k_ref, v_ref, o_ref, i_ref, f_ref,
                alog_ref, dskip_ref, snrm_ref, mnrm_ref,
                tril_ref, exp_h_ref, exp_v_ref, exp_k_ref, y_ref, h_ref, state, c_st, n_st, m_st):
    @pl.when(pl.program_id(1) == 0)
    def _():
        for st_ref in (state, c_st, n_st, m_st):
            st_ref[...] = jnp.zeros(st_ref.shape, F32)

    for bi in range(z_ref.shape[0]):
        one = lambda ref: ref.at[bi:bi + 1]
        _ssd_body(one(z_ref), one(xc_ref), one(dt_ref), alog_ref, dskip_ref, snrm_ref, tril_ref, exp_h_ref,
                  one(y_ref), state.at[bi])
        _mlstm_body(one(q_ref), one(k_ref), one(v_ref), one(o_ref), one(i_ref), one(f_ref), mnrm_ref,
                    tril_ref, exp_v_ref, exp_k_ref, one(h_ref), c_st.at[bi], n_st.at[bi], m_st.at[bi])


def _mixer(acts, params):
    b, l, _ = acts[0].shape
    kw = ML_HEADS * ML_QK_DIM
    consts = [_rhs_stack(np.tril(np.ones((CHUNK, CHUNK)))),
              _lhs_stack(_expander(LANES, SSD_WIDTH, SSD_HEAD_DIM)), _lhs_stack(_expander(LANES, ML_WIDTH, ML_V_DIM)),
              _lhs_stack(_expander(LANES, kw, ML_QK_DIM))]
    nb = MIXER_SEQS_PER_STEP if b % MIXER_SEQS_PER_STEP == 0 else 1
    blk = lambda w: pl.BlockSpec((nb, CHUNK, w), lambda i, c: (i, c, 0))
    return pl.pallas_call(
        _mixer_body,
        out_shape=[jax.ShapeDtypeStruct((b, l, SSD_WIDTH), BF16), jax.ShapeDtypeStruct((b, l, ML_WIDTH), BF16)],
        grid=(b // nb, l // CHUNK),
        in_specs=([blk(a.shape[2]) for a in acts] + [_resident(p.shape) for p in params]
                  + [_resident(c.shape) for c in consts]),
        out_specs=[blk(SSD_WIDTH), blk(ML_WIDTH)],
        scratch_shapes=[pltpu.VMEM((nb, SSD_STATE, SSD_WIDTH), F32),
                        pltpu.VMEM((nb, ML_QK_DIM, ML_WIDTH), F32),
                        pltpu.VMEM((nb, SUBLANES, kw), F32),
                        pltpu.VMEM((nb, SUBLANES, LANES), F32)],
        compiler_params=_params("parallel", "arbitrary"),
        name="mixer",
    )(*acts, *params, *consts)


def _rope(x, cos, sin, first_half):
    partner = jnp.where(first_half, pltpu.roll(x, LANES - ATT_HEAD_DIM // 2, 1), pltpu.roll(x, ATT_HEAD_DIM // 2, 1))
    return x * cos + partner * sin


def _sa_proj_body(x_ref, g_ref, w_ref, gq_ref, gk_ref, cs_ref, seg_ref, expand_ref,
                  qe_ref, qo_ref, qie_ref, qio_ref, ka_ref, kb_ref, vt_ref, wt_ref):
    tm = x_ref.shape[0]
    tk = vt_ref.shape[2]
    nq = ATT_HEADS * ATT_HEAD_DIM
    nqi = IDX_HEADS * IDX_DIM
    h = _rms(x_ref[...], g_ref[...]).astype(BF16)
    proj = _dot(h, w_ref[...])
    lane = _iota((tm, LANES), 1)
    half = ATT_HEAD_DIM // 2
    first_half = (lane % ATT_HEAD_DIM) < half
    low_head = lane < ATT_HEAD_DIM
    cs = cs_ref[...]
    cos = jnp.where(lane < half, cs, 0.0)
    cos = cos + pltpu.roll(cos, half, 1)
    cos = cos + pltpu.roll(cos, ATT_HEAD_DIM, 1)
    sin = jnp.where((lane >= half) & low_head, cs, 0.0)
    sin = sin - pltpu.roll(sin, LANES - half, 1)
    sin = sin + pltpu.roll(sin, ATT_HEAD_DIM, 1)

    q = proj[:, :nq]
    ms = _dot_f32_lhs(q * q, seg_ref[...]) * (1.0 / ATT_HEAD_DIM)
    rs = _dot_f32_lhs(lax.rsqrt(ms + EPS), expand_ref[...])
    qn = q * rs * gq_ref[...]
    q_scale = ATT_HEAD_DIM ** -0.5 * LOG2_E

    def emit(even_ref, odd_ref, slab, s0):
        zeros = jnp.zeros((ATT_HEAD_DIM, CHUNK), F32)
        for b0 in range(0, tm, CHUNK):
            feat = slab[b0:b0 + CHUNK].T
            even_ref[b0 // CHUNK, :, s0:s0 + LANES] = jnp.concatenate([feat[:ATT_HEAD_DIM], zeros], 0).astype(BF16)
            odd_ref[b0 // CHUNK, :, s0:s0 + LANES] = jnp.concatenate([zeros, feat[ATT_HEAD_DIM:]], 0).astype(BF16)

    for s0 in range(0, nq, LANES):
        emit(qe_ref, qo_ref, _rope(qn[:, s0:s0 + LANES], cos, sin, first_half) * q_scale, s0)
    for s0 in range(0, nqi, LANES):
        emit(qie_ref, qio_ref, _rope(proj[:, nq + s0:nq + s0 + LANES], cos, sin, first_half), s0)

    kk = proj[:, nq + nqi:nq + nqi + LANES]
    k_ms = jnp.sum(jnp.where(low_head, kk * kk, 0.0), axis=1, keepdims=True) * (1.0 / ATT_HEAD_DIM)
    kk = jnp.where(low_head, kk * lax.rsqrt(k_ms + EPS) * gk_ref[...], kk)
    kk = _rope(kk, cos, sin, first_half)
    ka_ref[...] = kk.astype(BF16)
    kb_ref[...] = pltpu.roll(kk, ATT_HEAD_DIM, 1).astype(BF16)
    vv = proj[:, nq + nqi + LANES:nq + nqi + 2 * LANES]
    for c in range(tm // tk):
        vt_ref[c] = vv[c * tk:(c + 1) * tk, :].T[:ATT_HEAD_DIM, :].astype(BF16)
    wt_ref[...] = proj[:, nq + nqi + 2 * LANES:].T[:SUBLANES, :]


def _sa_proj(x2d, g, w, gq, gk, rope, *, tm, tk):
    t, d = x2d.shape
    nq = ATT_HEADS * ATT_HEAD_DIM
    nqi = IDX_HEADS * IDX_DIM
    row = lambda wd: pl.BlockSpec((tm, wd), lambda i: (i, 0))
    qblocks = lambda wd: jax.ShapeDtypeStruct((t // CHUNK, LANES, wd), BF16)
    qspec = lambda wd: pl.BlockSpec((tm // CHUNK, LANES, wd), lambda i: (i, 0, 0))
    outs = [qblocks(nq), qblocks(nq), qblocks(nqi), qblocks(nqi),
            jax.ShapeDtypeStruct((t, LANES), BF16), jax.ShapeDtypeStruct((t, LANES), BF16),
            jax.ShapeDtypeStruct((t // tk, ATT_HEAD_DIM, tk), BF16),
            jax.ShapeDtypeStruct((SUBLANES, t), F32)]
    head_of_lane = _expander(LANES, nq, ATT_HEAD_DIM)
    expand, seg = _lhs_stack(head_of_lane), _lhs_stack(head_of_lane.T)
    return pl.pallas_call(
        _sa_proj_body,
        out_shape=outs,
        grid=(t // tm,),
        in_specs=[row(d), _resident(g.shape), _resident(w.shape), _resident(gq.shape), _resident(gk.shape),
                  row(LANES), _resident(seg.shape), _resident(expand.shape)],
        out_specs=[qspec(nq), qspec(nq), qspec(nqi), qspec(nqi), row(LANES), row(LANES),
                   pl.BlockSpec((tm // tk, ATT_HEAD_DIM, tk), lambda i: (i, 0, 0)),
                   pl.BlockSpec((SUBLANES, tm), lambda i: (0, i))],
        compiler_params=_params("parallel"),
        name="sa_proj",
    )(x2d, g, w, gq, gk, rope, seg, expand)


INT_MIN = -2 ** 31
NEG_INF_KEY = INT_MIN + 0x7FFFFF
ATT_KEY_CHUNK = 256


def _sortable_key(x):
    bits = pltpu.bitcast(jnp.where(x == 0.0, 0.0, x), I32)
    return bits ^ ((bits >> 31) & 0x7FFFFFFF)


def _dsa_body(qe_ref, qo_ref, qie_ref, qio_ref, wt_ref, ka_ref, kb_ref, vt_ref, o_ref,
              skey, last_s, m_s, l_s, a_s, acc_s, p_s, *, topk, seq_bits):
    blk = CHUNK
    tk = vt_ref.shape[2]
    j = pl.program_id(1)
    n_chunks = lax.div(j * blk + blk + tk - 1, tk)
    row = _iota((tk, LANES), 0)
    q_pos = j * blk + _iota((tk, LANES), 1)

    qe, qo, qie, qio = qe_ref.at[0], qo_ref.at[0], qie_ref.at[0], qio_ref.at[0]
    n_pairs = ATT_HEADS // 2
    n_ipairs = IDX_HEADS // 2

    idx_scale = (IDX_HEADS ** -0.5) * (IDX_DIM ** -0.5)
    wt = wt_ref[...]

    def chunk(ci):
        return pl.ds(pl.multiple_of(ci * tk, tk), tk)

    cw = 2 * LANES

    def score_chunk(ci):
        sc = jnp.zeros((tk, LANES), F32)
        for par, (src, k_ref) in enumerate(((qie, kb_ref), (qio, ka_ref))):
            kc = k_ref[0, chunk(ci), :]
            for c0 in range(0, n_ipairs * blk, cw):
                logits = _dot(kc, src[:, c0:c0 + cw])
                for u in range(cw // blk):
                    head = 2 * (c0 // blk + u) + par
                    sc = sc + jnp.maximum(logits[:, u * blk:(u + 1) * blk], 0.0) * wt[head:head + 1, :]
        sc = sc * idx_scale
        sc = jnp.where(ci * tk + row <= q_pos, sc, -jnp.inf)
        skey[chunk(ci), :] = _sortable_key(sc)

    def score_chunk_pair(i, _):
        score_chunk(2 * i)
        score_chunk(2 * i + 1)
        return 0

    lax.fori_loop(0, lax.shift_right_logical(n_chunks, 1), score_chunk_pair, 0)

    @pl.when((n_chunks & 1) == 1)
    def _():
        score_chunk(n_chunks - 1)

    acc_rows = 4 * SUBLANES

    def count(pred_fn):
        def body(ci, c):
            hit = jnp.where(pred_fn(skey[chunk(ci), :], ci), 1.0, 0.0)
            return c + jnp.sum(hit.reshape(tk // acc_rows, acc_rows, LANES), axis=0)
        part = lax.fori_loop(0, n_chunks, body, jnp.zeros((acc_rows, LANES), F32))
        return jnp.sum(part, axis=0, keepdims=True)

    kf = float(topk)
    thr = jnp.full((1, LANES), INT_MIN, I32)
    n_ge = jnp.full((1, LANES), 1.0, F32) * (n_chunks * tk).astype(F32)
    for bit in range(31, -1, -1):
        cand = thr + jnp.int32(INT_MIN if bit == 31 else 1 << bit)
        c = count(lambda key, ci, cand=cand: key >= cand)
        thr = jnp.where(c >= kf, cand, thr)
        n_ge = jnp.where(c >= kf, c, n_ge)

    last_s[...] = jnp.full(last_s.shape, 2 ** 31 - 1, I32)
    tied = (n_ge > kf) & (thr > NEG_INF_KEY)

    @pl.when(jnp.sum(jnp.where(tied, 1.0, 0.0)) > 0.0)
    def _():
        need = kf - count(lambda key, ci: key > thr)
        last = jnp.zeros((1, LANES), I32)
        for bit in range(seq_bits - 1, -1, -1):
            cand = last + jnp.int32(1 << bit)
            c = count(lambda key, ci, cand=cand: (key == thr) & (ci * tk + row < cand))
            last = jnp.where(c <= need - 1.0, cand, last)
        last_s[...] = jnp.broadcast_to(last, last_s.shape)

    last = last_s[0:1, :]

    m_s[...] = jnp.full(m_s.shape, -1e30, F32)
    l_s[...] = jnp.zeros(l_s.shape, F32)
    acc_s[...] = jnp.zeros(acc_s.shape, F32)
    half = n_pairs * blk

    def masked_logits(ci):
        key = skey[chunk(ci), :]
        pos = ci * tk + row
        sel = ((key > thr) | ((key == thr) & (pos <= last))) & (pos <= q_pos)
        bias = jnp.where(sel, 0.0, -jnp.inf)
        bias = jnp.concatenate([bias] * (cw // blk), axis=1)
        for src, k_ref, base in ((qe, ka_ref, 0), (qo, kb_ref, half)):
            kc = k_ref[0, chunk(ci), :]
            for c0 in range(0, half, cw):
                yield base + c0, _dot(kc, src[:, c0:c0 + cw]) + bias

    def probabilities(ci):
        m_chunk = [jnp.max(s, axis=0, keepdims=True) for _, s in masked_logits(ci)]
        m_prev = m_s[...]
        m_cur = jnp.maximum(m_prev, jnp.concatenate(m_chunk, axis=1))
        alpha = jnp.exp2(m_prev - m_cur)
        l_chunk = []
        for c0, s in masked_logits(ci):
            p = jnp.exp2(s - m_cur[:, c0:c0 + cw])
            l_chunk.append(jnp.sum(p, axis=0, keepdims=True))
            p_s[:, c0:c0 + cw] = p.astype(BF16)
        l_s[...] = alpha * l_s[...] + jnp.concatenate(l_chunk, axis=1)
        m_s[...] = m_cur
        a_s[...] = alpha

    def accumulate(ci):
        acc_s[...] = a_s[...] * acc_s[...] + _dot(vt_ref[ci], p_s[...])

    def attn_chunk(ci):
        accumulate(ci - 1)
        probabilities(ci)

    def attn_chunk_pair(i, _):
        attn_chunk(2 * i + 1)
        attn_chunk(2 * i + 2)
        return 0

    probabilities(jnp.int32(0))
    n_rest = n_chunks - 1
    lax.fori_loop(0, lax.shift_right_logical(n_rest, 1), attn_chunk_pair, 0)

    @pl.when((n_rest & 1) == 1)
    def _():
        attn_chunk(n_rest)

    accumulate(n_rest)

    out = acc_s[...] / l_s[...]
    for p in range(n_pairs):
        pair = jnp.concatenate([out[:, p * blk:(p + 1) * blk], out[:, half + p * blk:half + (p + 1) * blk]], axis=0)
        o_ref[0, :, p * LANES:(p + 1) * LANES] = pair.T.astype(o_ref.dtype)


def _dsa(qe, qo, qie, qio, wt, ka, kb, vt, *, topk):
    b, l, _ = ka.shape
    nb = l // CHUNK
    tk = vt.shape[2]
    rows = (ATT_HEADS // 2) * CHUNK
    nq = ATT_HEADS * ATT_HEAD_DIM
    qop = lambda a: pl.BlockSpec((1,) + a.shape[1:], lambda i, j: (i * nb + j, 0, 0))
    seq = pl.BlockSpec((1, l, LANES), lambda i, j: (i, 0, 0))
    return pl.pallas_call(
        functools.partial(_dsa_body, topk=topk, seq_bits=int(np.log2(l))),
        out_shape=jax.ShapeDtypeStruct((b, l, nq), BF16),
        grid=(b, nb),
        in_specs=[qop(qe), qop(qo), qop(qie), qop(qio),
                  pl.BlockSpec((SUBLANES, CHUNK), lambda i, j: (0, i * nb + j)),
                  seq, seq, pl.BlockSpec((l // tk, ATT_HEAD_DIM, tk), lambda i, j: (i, 0, 0))],
        out_specs=pl.BlockSpec((1, CHUNK, nq), lambda i, j: (i, j, 0)),
        scratch_shapes=[pltpu.VMEM((l, LANES), I32), pltpu.VMEM((SUBLANES, LANES), I32),
                        pltpu.VMEM((1, 2 * rows), F32), pltpu.VMEM((1, 2 * rows), F32),
                        pltpu.VMEM((1, 2 * rows), F32), pltpu.VMEM((ATT_HEAD_DIM, 2 * rows), F32),
                        pltpu.VMEM((tk, 2 * rows), BF16)],
        compiler_params=_params("parallel", "arbitrary"),
        name="dsa",
    )(qe, qo, qie, qio, wt, ka, kb, vt)


def _split_cols(w, sizes):
    out, c0 = [], 0
    for s in sizes:
        out.append(w[:, c0:c0 + s])
        c0 += s
    return out


def _pad_cols(w, width):
    return jnp.pad(w, ((0, 0), (0, width - w.shape[1])))


def _pad_row(v, width):
    return jnp.pad(v, (0, width - v.shape[0]))[None, :]


def _hybrid_mixer_parts(x2d, b, l, mix_norm, w_in, conv_w, conv_b, dt_bias, a_log, d_skip, ssd_norm,
                        igate_bias, fgate_bias, mlstm_norm, *, tm):
    wz, wxbc, wdt, wq, wk, wv, wo, wi, wf = _split_cols(w_in, HY_SPLITS)
    w = jnp.concatenate([wz, wxbc, wq, wk, wv, wo, _pad_cols(wdt, LANES), _pad_cols(wi, LANES),
                         _pad_cols(wf, LANES)], axis=1).astype(BF16)
    z, xc, qm, km, vm, om, dts, ig, fg = _hy_proj(
        x2d, mix_norm[None, :], w, conv_w, conv_b[None, :], _pad_row(dt_bias, LANES), _pad_row(igate_bias, LANES),
        _pad_row(fgate_bias, LANES), l, tm=tm)
    acts = [a.reshape(b, l, a.shape[1]) for a in (z, xc, dts, qm, km, vm, om, ig, fg)]
    params = [_pad_row(a_log, LANES), jnp.repeat(d_skip, SSD_HEAD_DIM)[None, :], ssd_norm[None, :],
              mlstm_norm[None, :]]
    y, hm = _mixer(acts, params)
    return y.reshape(b * l, SSD_WIDTH), hm.reshape(b * l, ML_WIDTH)


def _rope_table(positions):
    inv = ROPE_THETA ** (-jnp.arange(0, ATT_HEAD_DIM, 2, dtype=F32) / ATT_HEAD_DIM)
    ang = positions.astype(F32).reshape(-1, 1) * inv
    cos, sin = lax.optimization_barrier((jnp.cos(ang), jnp.sin(ang)))
    return jnp.concatenate([cos, sin, jnp.zeros_like(cos), jnp.zeros_like(cos)], axis=1)


def _sparse_attention_parts(x2d, b, l, rope, mix_norm, w_in, q_norm, k_norm, topk, *, tm):
    wq, wk, wv, wqi, wki, wwi = _split_cols(w_in, SA_SPLITS)
    w = jnp.concatenate([wq, wqi, wk, wki, _pad_cols(wv, LANES), _pad_cols(wwi, LANES)], axis=1).astype(BF16)
    gq = jnp.tile(q_norm, ATT_HEADS)[None, :]
    gk = _pad_row(k_norm, LANES)
    qe, qo, qie, qio, ka, kb, vt, wt = _sa_proj(x2d, mix_norm[None, :], w, gq, gk, rope, tm=tm, tk=ATT_KEY_CHUNK)
    r3 = lambda a: a.reshape(b, l, a.shape[1])
    o = _dsa(qe, qo, qie, qio, wt, r3(ka), r3(kb), vt, topk=topk)
    return o.reshape(b * l, ATT_HEADS * ATT_HEAD_DIM)


def kernel(x, positions, ffn_norm, ffn_w_gate, ffn_w_up, ffn_w_down, mix_norm, hy_w_in, hy_conv_w, hy_conv_b,
           hy_dt_bias, hy_a_log, hy_d_skip, hy_ssd_norm, hy_igate_bias, hy_fgate_bias, hy_mlstm_norm, hy_w_out,
           sa_w_in, sa_q_norm, sa_k_norm, sa_w_out):
    b, l, d = x.shape
    depth = ffn_norm.shape[0]
    t = b * l
    tm = 512 if t % 512 == 0 else CHUNK
    topk = min(TOPK_MAX, l // 4)
    rope = _rope_table(positions)
    wg, wu, wd = ffn_w_gate, ffn_w_up, ffn_w_down
    x2d = x.reshape(t, d)
    for layer in range(depth):
        x2d = _ffn(x2d, ffn_norm[layer, 0][None, :], wg, wu, wd, layer, 0, tm=tm)
        if layer % 2 == 0:
            e = layer // 2
            y, hm = _hybrid_mixer_parts(x2d, b, l, mix_norm[layer], hy_w_in[e], hy_conv_w[e], hy_conv_b[e],
                                        hy_dt_bias[e], hy_a_log[e], hy_d_skip[e], hy_ssd_norm[e],
                                        hy_igate_bias[e], hy_fgate_bias[e], hy_mlstm_norm[e], tm=tm)
            w_out = hy_w_out[e].astype(BF16)
            acts, ws = [y, hm], [w_out[:SSD_WIDTH], w_out[SSD_WIDTH:]]
        else:
            o = layer // 2
            att = _sparse_attention_parts(x2d, b, l, rope, mix_norm[layer], sa_w_in[o], sa_q_norm[o],
                                          sa_k_norm[o], topk, tm=tm)
            acts, ws = [att], [sa_w_out[o].astype(BF16)]
        x2d = _proj_ffn(x2d, acts, ws, ffn_norm[layer, 1][None, :], wg, wu, wd, layer, 1, tm=tm)
    return x2d.reshape(b, l, d)
```

```python
import functools

import jax
import jax.numpy as jnp
import numpy as np
from jax import lax
from jax.experimental import pallas as pl
from jax.experimental.pallas import tpu as pltpu

F32 = jnp.float32
BF16 = jnp.bfloat16
I32 = jnp.int32

EPS = 1e-6
ROPE_THETA = 10000.0
LOG2_E = 1.4426950408889634
LANES = 128
SUBLANES = 8
CHUNK = 128
CONV_K = 4
TOPK_MAX = 256

D_MODEL = 1024
D_FF = 2816
SSD_WIDTH = 1024
SSD_HEAD_DIM = 64
SSD_HEADS = 16
SSD_GROUPS = 2
SSD_STATE = 128
SSD_CONV_DIM = SSD_WIDTH + 2 * SSD_GROUPS * SSD_STATE
ML_WIDTH = 1024
ML_HEADS = 4
ML_V_DIM = 256
ML_QK_DIM = 128
HY_SPLITS = (SSD_WIDTH, SSD_CONV_DIM, SSD_HEADS, ML_HEADS * ML_QK_DIM, ML_HEADS * ML_QK_DIM,
             ML_WIDTH, ML_WIDTH, ML_HEADS, ML_HEADS)
ATT_HEAD_DIM = 64
ATT_HEADS = 16
IDX_HEADS = 8
IDX_DIM = 64
SA_SPLITS = (ATT_HEADS * ATT_HEAD_DIM, ATT_HEAD_DIM, ATT_HEAD_DIM, IDX_HEADS * IDX_DIM, IDX_DIM, IDX_HEADS)

VMEM_LIMIT_BYTES = 56 * 1024 * 1024

NT_DIMS = (((1,), (1,)), ((), ()))


def _params(*sem):
    return pltpu.CompilerParams(dimension_semantics=sem, vmem_limit_bytes=VMEM_LIMIT_BYTES)


def _resident(shape):
    nd = len(shape)
    return pl.BlockSpec(shape, lambda *_: (0,) * nd, pipeline_mode=pl.Buffered(1))


def _dot(a, b):
    return jnp.dot(a, b, preferred_element_type=F32)


def _dot_nt(a, b):
    return lax.dot_general(a, b, NT_DIMS, preferred_element_type=F32)


def _split3(a):
    a1 = a.astype(BF16)
    r = a - a1.astype(F32)
    a2 = r.astype(BF16)
    r = r - a2.astype(F32)
    return a1, a2, r.astype(BF16)


N_SPLIT = 3


def _dot_f32_lhs(a, b01_stack):
    return _dot(jnp.concatenate(_split3(a), axis=1), b01_stack)


def _dot_f32_rhs(a01_stack, b):
    return _dot(a01_stack, jnp.concatenate(_split3(b), axis=0))


def _rms(x, g):
    return x * lax.rsqrt(jnp.mean(x * x, axis=-1, keepdims=True) + EPS) * g


def _sigmoid(x):
    return 1.0 / (1.0 + jnp.exp(-x))


def _softplus(x):
    return jnp.maximum(x, 0.0) + jnp.log1p(jnp.exp(-jnp.abs(x)))


def _iota(shape, dim):
    return lax.broadcasted_iota(I32, shape, dim)


def _expander(rows, cols, width):
    return np.arange(cols)[None, :] // width == np.arange(rows)[:, None]


def _lhs_stack(b01):
    return jnp.asarray(np.tile(b01, (N_SPLIT, 1)), dtype=BF16)


def _rhs_stack(a01):
    return jnp.asarray(np.tile(a01, (1, N_SPLIT)), dtype=BF16)


def _stacked(w, layer, idx):
    return pl.BlockSpec((None, None) + w.shape[2:], lambda *_: (layer, idx, 0, 0), pipeline_mode=pl.Buffered(1))


def _tril(n):
    return (_iota((n, n), 0) >= _iota((n, n), 1))


def _swiglu_half(x, g_ref, wg_ref, wu_ref, wd_ref, ff_chunk):
    h = _rms(x, g_ref[...]).astype(BF16)
    acc = jnp.zeros(x.shape, F32)
    for c0 in range(0, wg_ref.shape[1], ff_chunk):
        gate = _dot(h, wg_ref[:, c0:c0 + ff_chunk].astype(BF16))
        up = _dot(h, wu_ref[:, c0:c0 + ff_chunk].astype(BF16))
        act = (gate * _sigmoid(gate) * up).astype(BF16)
        acc = acc + _dot(act, wd_ref[c0:c0 + ff_chunk, :].astype(BF16))
    return x + 0.5 * acc


def _ffn_body(x_ref, g_ref, wg_ref, wu_ref, wd_ref, o_ref, *, ff_chunk):
    o_ref[...] = _swiglu_half(x_ref[...], g_ref, wg_ref, wu_ref, wd_ref, ff_chunk)


def _ffn(x2d, g, wg, wu, wd, layer, idx, *, tm, ff_chunk=256):
    t, d = x2d.shape
    return pl.pallas_call(
        functools.partial(_ffn_body, ff_chunk=ff_chunk),
        out_shape=jax.ShapeDtypeStruct((t, d), F32),
        grid=(t // tm,),
        in_specs=[pl.BlockSpec((tm, d), lambda i: (i, 0)), _resident(g.shape),
                  _stacked(wg, layer, idx), _stacked(wu, layer, idx), _stacked(wd, layer, idx)],
        out_specs=pl.BlockSpec((tm, d), lambda i: (i, 0)),
        compiler_params=_params("parallel"),
        name="ffn",
    )(x2d, g, wg, wu, wd)


def _proj_ffn_body(*refs, n_in, ff_chunk):
    x_ref = refs[0]
    a_refs = refs[1:1 + n_in]
    w_refs = refs[1 + n_in:1 + 2 * n_in]
    g_ref, wg_ref, wu_ref, wd_ref, o_ref = refs[1 + 2 * n_in:]
    x = x_ref[...]
    for a_ref, w_ref in zip(a_refs, w_refs):
        x = x + _dot(a_ref[...], w_ref[...])
    o_ref[...] = _swiglu_half(x, g_ref, wg_ref, wu_ref, wd_ref, ff_chunk)


def _proj_ffn(x2d, acts, ws, g, wg, wu, wd, layer, idx, *, tm, ff_chunk=256):
    t, d = x2d.shape
    n_in = len(acts)
    return pl.pallas_call(
        functools.partial(_proj_ffn_body, n_in=n_in, ff_chunk=ff_chunk),
        out_shape=jax.ShapeDtypeStruct((t, d), F32),
        grid=(t // tm,),
        in_specs=([pl.BlockSpec((tm, d), lambda i: (i, 0))]
                  + [pl.BlockSpec((tm, a.shape[1]), lambda i: (i, 0)) for a in acts]
                  + [_resident(w.shape) for w in ws]
                  + [_resident(g.shape), _stacked(wg, layer, idx), _stacked(wu, layer, idx),
                     _stacked(wd, layer, idx)]),
        out_specs=pl.BlockSpec((tm, d), lambda i: (i, 0)),
        compiler_params=_params("parallel"),
        name="proj_ffn",
    )(x2d, *acts, *ws, g, wg, wu, wd)


def _hy_proj_body(x_ref, g_ref, w_ref, cw_ref, cb_ref, dtb_ref, ib_ref, fb_ref, z_ref, xc_ref, q_ref, k_ref, v_ref,
                  o_ref, dt_ref, i_ref, f_ref, tails, *, tiles_per_seq):
    tm = x_ref.shape[0]
    lane = _iota((tm, LANES), 1)
    ssd_head, ml_head = lane < SSD_HEADS, lane < ML_HEADS
    same = lambda a: a
    outputs = (
        (z_ref, lambda a: a * _sigmoid(a)),
        (None, None),
        (q_ref, lambda a: a * (ML_QK_DIM ** -0.5)),
        (k_ref, same),
        (v_ref, same),
        (o_ref, _sigmoid),
        (dt_ref, lambda a: jnp.where(ssd_head, _softplus(a + dtb_ref[...]), 0.0)),
        (i_ref, lambda a: jnp.where(ml_head, a + ib_ref[...], 0.0)),
        (f_ref, lambda a: jnp.where(ml_head, -_softplus(-(a + fb_ref[...])), 0.0)),
    )

    @pl.when(pl.program_id(0) % tiles_per_seq == 0)
    def _():
        tails[...] = jnp.zeros(tails.shape, F32)

    h = _rms(x_ref[...], g_ref[...]).astype(BF16)

    u = _dot(h, w_ref[:, SSD_WIDTH:SSD_WIDTH + SSD_CONV_DIM])
    first_row = _iota((SUBLANES, SSD_CONV_DIM), 0) == 0
    part = cw_ref[0:1, :] * u
    for k in range(1, CONV_K):
        rolled = pltpu.roll(part, 1, 0)
        head = jnp.where(first_row, pltpu.roll(tails[k - 1], 1, 0), rolled[:SUBLANES])
        tails[k - 1] = part[tm - SUBLANES:, :]
        part = jnp.concatenate([head, rolled[SUBLANES:]], axis=0) + cw_ref[k:k + 1, :] * u
    conv = part + cb_ref[...]
    xc_ref[...] = (conv * _sigmoid(conv)).astype(xc_ref.dtype)

    c0 = 0
    for out_ref, finish in outputs:
        if out_ref is None:
            c0 += SSD_CONV_DIM
            continue
        wd = out_ref.shape[1]
        out_ref[...] = finish(_dot(h, w_ref[:, c0:c0 + wd])).astype(out_ref.dtype)
        c0 += wd


def _hy_proj(x2d, g, w, conv_w, conv_b, dt_bias, ib, fb, seq_len, *, tm):
    t, d = x2d.shape
    kw = ML_HEADS * ML_QK_DIM
    widths = (SSD_WIDTH, SSD_CONV_DIM, kw, kw, ML_WIDTH, ML_WIDTH, LANES, LANES, LANES)
    return pl.pallas_call(
        functools.partial(_hy_proj_body, tiles_per_seq=seq_len // tm),
        out_shape=[jax.ShapeDtypeStruct((t, wd), F32) for wd in widths],
        grid=(t // tm,),
        in_specs=[pl.BlockSpec((tm, d), lambda i: (i, 0)), _resident(g.shape), _resident(w.shape),
                  _resident(conv_w.shape), _resident(conv_b.shape), _resident(dt_bias.shape),
                  _resident(ib.shape), _resident(fb.shape)],
        out_specs=[pl.BlockSpec((tm, wd), lambda i: (i, 0)) for wd in widths],
        scratch_shapes=[pltpu.VMEM((CONV_K - 1, SUBLANES, SSD_CONV_DIM), F32)],
        compiler_params=_params("arbitrary"),
        name="hy_proj",
    )(x2d, g, w, conv_w, conv_b, dt_bias, ib, fb)


def _ssd_body(zg_ref, xc_ref, dt_ref, alog_ref, dskip_ref, nrm_ref, tril_ref, expand_ref, y_ref, state):
    q_len = CHUNK
    hp = SSD_WIDTH // SSD_GROUPS
    xc = xc_ref[0]
    xs = xc[:, :SSD_WIDTH]
    bm = xc[:, SSD_WIDTH:SSD_WIDTH + SSD_GROUPS * SSD_STATE]
    cm = xc[:, SSD_WIDTH + SSD_GROUPS * SSD_STATE:]

    dt = dt_ref[0]
    a = dt * (-jnp.exp(alog_ref[...]))
    tril = _tril(q_len)
    acs = _dot_f32_rhs(tril_ref[...], a)
    eacs = jnp.exp(acs)
    dend = jnp.exp(acs[q_len - 1:q_len, :] - acs)
    wide = _dot_f32_lhs(jnp.concatenate([dt, eacs, dend], axis=0), expand_ref[...])
    dt_x, eacs_x, dend_x = wide[:q_len], wide[q_len:2 * q_len], wide[2 * q_len:]

    xd = xs * dt_x
    lane_w = _iota((q_len, SSD_WIDTH), 1)
    low_half = (lane_w % LANES) < SSD_HEAD_DIM
    xd_halves = (jnp.where(low_half, xd, 0.0).astype(BF16), jnp.where(low_half, 0.0, xd).astype(BF16))
    xdd_b = (xd * dend_x).astype(BF16)
    acs_t = acs.T
    st = state[...]
    st_b = st.astype(BF16)

    y_diag, y_off, s_new = [], [], []
    heads_per_group = SSD_HEADS // SSD_GROUPS
    for g in range(SSD_GROUPS):
        bg = bm[:, g * SSD_STATE:(g + 1) * SSD_STATE]
        cg_b = cm[:, g * SSD_STATE:(g + 1) * SSD_STATE].astype(BF16)
        cb = _dot_nt(cg_b, bg.astype(BF16))
        s_new.append(_dot(bg.T.astype(BF16), xdd_b[:, g * hp:(g + 1) * hp]))
        y_off.append(_dot(cg_b, st_b[:, g * hp:(g + 1) * hp]))
        for pair in range(heads_per_group // 2):
            acc = None
            for par in range(2):
                h = g * heads_per_group + 2 * pair + par
                seg = acs[:, h:h + 1] - acs_t[h:h + 1, :]
                m = (cb * jnp.exp(jnp.where(tril, seg, -jnp.inf))).astype(BF16)
                slab = (h // 2) * LANES
                d = _dot(m, xd_halves[par][:, slab:slab + LANES])
                acc = d if acc is None else acc + d
            y_diag.append(acc)
    y = (jnp.concatenate(y_diag, axis=1) + eacs_x * jnp.concatenate(y_off, axis=1) + xs * dskip_ref[...])
    state[...] = st * eacs_x[q_len - 1:q_len, :] + jnp.concatenate(s_new, axis=1)

    yg = y * zg_ref[0]
    nrm = nrm_ref[...]
    y_ref[0] = jnp.concatenate(
        [_rms(yg[:, g * hp:(g + 1) * hp], nrm[:, g * hp:(g + 1) * hp]) for g in range(SSD_GROUPS)],
        axis=1).astype(y_ref.dtype)


def _mlstm_body(q_ref, k_ref, v_ref, og_ref, i_ref, f_ref, nrm_ref, tril_ref, exp_v_ref, exp_k_ref,
                h_ref, c_st, n_st, m_st):
    q_len = CHUNK

    li = i_ref[0]
    lf = f_ref[0]
    tril = _tril(q_len)
    bcs = _dot_f32_rhs(tril_ref[...], lf)
    b_last = bcs[q_len - 1:q_len, :]
    gl = b_last - bcs + li
    m_loc = jnp.max(gl, axis=0, keepdims=True)
    wg = jnp.exp(gl - m_loc)
    m_in = m_st[0:1, :]
    m_new = jnp.maximum(b_last + m_in, m_loc)
    a_sc = jnp.exp(b_last + m_in - m_new)
    g_sc = jnp.exp(m_loc - m_new)

    sc = jnp.concatenate([wg, jnp.broadcast_to(a_sc, (SUBLANES, LANES)),
                          jnp.broadcast_to(g_sc, (SUBLANES, LANES))], axis=0)
    sc_v = _dot_f32_lhs(sc, exp_v_ref[...])
    sc_k = _dot_f32_lhs(sc, exp_k_ref[...])
    wg_v, a_v, g_v = sc_v[:q_len], sc_v[q_len:q_len + 1], sc_v[q_len + SUBLANES:q_len + SUBLANES + 1]
    wg_k, a_k, g_k = sc_k[:q_len], sc_k[q_len:q_len + 1], sc_k[q_len + SUBLANES:q_len + SUBLANES + 1]

    q = q_ref[0]
    k = k_ref[0]
    v = v_ref[0]
    q_b, k_b, v_b = q.astype(BF16), k.astype(BF16), v.astype(BF16)
    vw_b = (v * wg_v).astype(BF16)
    n_in = n_st[0:1, :]
    n_loc = jnp.sum(k * wg_k, axis=0, keepdims=True)
    d_t = (li - bcs).T

    outs, c_new = [], []
    for h in range(ML_HEADS):
        ks = slice(h * ML_QK_DIM, (h + 1) * ML_QK_DIM)
        vs = slice(h * ML_V_DIM, (h + 1) * ML_V_DIM)
        c_new.append(_dot(k[:, ks].T.astype(BF16), vw_b[:, vs]))
        dm = jnp.where(tril, bcs[:, h:h + 1] + d_t[h:h + 1, :], -jnp.inf)
        inter = bcs[:, h:h + 1] + m_in[:, h:h + 1]
        m_t = jnp.maximum(jnp.max(dm, axis=1, keepdims=True), inter)
        s = _dot_nt(q_b[:, ks], k_b[:, ks]) * jnp.exp(dm - m_t)
        w_int = jnp.exp(inter - m_t)
        num = _dot(s.astype(BF16), v_b[:, vs]) + w_int * _dot(q_b[:, ks], c_st[:, vs].astype(BF16))
        den = (jnp.sum(s, axis=1, keepdims=True)
               + w_int * jnp.sum(q[:, ks] * n_in[:, ks], axis=1, keepdims=True))
        outs.append(num / jnp.maximum(jnp.abs(den), jnp.exp(-m_t)))
    nrm = nrm_ref[...]
    hn = jnp.concatenate([_rms(outs[h], nrm[:, h * ML_V_DIM:(h + 1) * ML_V_DIM]) for h in range(ML_HEADS)],
                         axis=1)
    h_ref[0] = (hn * og_ref[0]).astype(h_ref.dtype)

    c_st[...] = a_v * c_st[...] + g_v * jnp.concatenate(c_new, axis=1)
    n_st[...] = jnp.broadcast_to(a_k * n_in + g_k * n_loc, n_st.shape)
    m_st[...] = jnp.broadcast_to(m_new, m_st.shape)


MIXER_SEQS_PER_STEP = 4


def _mixer_body(z_ref, xc_ref, dt_ref, q_ref, k_ref, v_ref, o_ref, i_ref, f_ref,
                alog_ref, dskip_ref, snrm_ref, mnrm_ref,
                tril_ref, exp_h_ref, exp_v_ref, exp_k_ref, y_ref, h_ref, state, c_st, n_st, m_st):
    @pl.when(pl.program_id(1) == 0)
    def _():
        for st_ref in (state, c_st, n_st, m_st):
            st_ref[...] = jnp.zeros(st_ref.shape, F32)

    for bi in range(z_ref.shape[0]):
        one = lambda ref: ref.at[bi:bi + 1]
        _ssd_body(one(z_ref), one(xc_ref), one(dt_ref), alog_ref, dskip_ref, snrm_ref, tril_ref, exp_h_ref,
                  one(y_ref), state.at[bi])
        _mlstm_body(one(q_ref), one(k_ref), one(v_ref), one(o_ref), one(i_ref), one(f_ref), mnrm_ref,
                    tril_ref, exp_v_ref, exp_k_ref, one(h_ref), c_st.at[bi], n_st.at[bi], m_st.at[bi])


def _mixer(acts, params):
    b, l, _ = acts[0].shape
    kw = ML_HEADS * ML_QK_DIM
    consts = [_rhs_stack(np.tril(np.ones((CHUNK, CHUNK)))),
              _lhs_stack(_expander(LANES, SSD_WIDTH, SSD_HEAD_DIM)), _lhs_stack(_expander(LANES, ML_WIDTH, ML_V_DIM)),
              _lhs_stack(_expander(LANES, kw, ML_QK_DIM))]
    nb = MIXER_SEQS_PER_STEP if b % MIXER_SEQS_PER_STEP == 0 else 1
    blk = lambda w: pl.BlockSpec((nb, CHUNK, w), lambda i, c: (i, c, 0))
    return pl.pallas_call(
        _mixer_body,
        out_shape=[jax.ShapeDtypeStruct((b, l, SSD_WIDTH), BF16), jax.ShapeDtypeStruct((b, l, ML_WIDTH), BF16)],
        grid=(b // nb, l // CHUNK),
        in_specs=([blk(a.shape[2]) for a in acts] + [_resident(p.shape) for p in params]
                  + [_resident(c.shape) for c in consts]),
        out_specs=[blk(SSD_WIDTH), blk(ML_WIDTH)],
        scratch_shapes=[pltpu.VMEM((nb, SSD_STATE, SSD_WIDTH), F32),
                        pltpu.VMEM((nb, ML_QK_DIM, ML_WIDTH), F32),
                        pltpu.VMEM((nb, SUBLANES, kw), F32),
                        pltpu.VMEM((nb, SUBLANES, LANES), F32)],
        compiler_params=_params("parallel", "arbitrary"),
        name="mixer",
    )(*acts, *params, *consts)


def _rope(x, cos, sin, first_half):
    partner = jnp.where(first_half, pltpu.roll(x, LANES - ATT_HEAD_DIM // 2, 1), pltpu.roll(x, ATT_HEAD_DIM // 2, 1))
    return x * cos + partner * sin


def _sa_proj_body(x_ref, g_ref, w_ref, gq_ref, gk_ref, cs_ref, seg_ref, expand_ref,
                  qe_ref, qo_ref, qie_ref, qio_ref, ka_ref, kb_ref, vt_ref, wt_ref):
    tm = x_ref.shape[0]
    tk = vt_ref.shape[2]
    nq = ATT_HEADS * ATT_HEAD_DIM
    nqi = IDX_HEADS * IDX_DIM
    h = _rms(x_ref[...], g_ref[...]).astype(BF16)
    proj = _dot(h, w_ref[...])
    lane = _iota((tm, LANES), 1)
    half = ATT_HEAD_DIM // 2
    first_half = (lane % ATT_HEAD_DIM) < half
    low_head = lane < ATT_HEAD_DIM
    cs = cs_ref[...]
    cos = jnp.where(lane < half, cs, 0.0)
    cos = cos + pltpu.roll(cos, half, 1)
    cos = cos + pltpu.roll(cos, ATT_HEAD_DIM, 1)
    sin = jnp.where((lane >= half) & low_head, cs, 0.0)
    sin = sin - pltpu.roll(sin, LANES - half, 1)
    sin = sin + pltpu.roll(sin, ATT_HEAD_DIM, 1)

    q = proj[:, :nq]
    ms = _dot_f32_lhs(q * q, seg_ref[...]) * (1.0 / ATT_HEAD_DIM)
    rs = _dot_f32_lhs(lax.rsqrt(ms + EPS), expand_ref[...])
    qn = q * rs * gq_ref[...]
    q_scale = ATT_HEAD_DIM ** -0.5 * LOG2_E

    def emit(even_ref, odd_ref, slab, s0):
        zeros = jnp.zeros((ATT_HEAD_DIM, CHUNK), F32)
        for b0 in range(0, tm, CHUNK):
            feat = slab[b0:b0 + CHUNK].T
            even_ref[b0 // CHUNK, :, s0:s0 + LANES] = jnp.concatenate([feat[:ATT_HEAD_DIM], zeros], 0).astype(BF16)
            odd_ref[b0 // CHUNK, :, s0:s0 + LANES] = jnp.concatenate([zeros, feat[ATT_HEAD_DIM:]], 0).astype(BF16)

    for s0 in range(0, nq, LANES):
        emit(qe_ref, qo_ref, _rope(qn[:, s0:s0 + LANES], cos, sin, first_half) * q_scale, s0)
    for s0 in range(0, nqi, LANES):
        emit(qie_ref, qio_ref, _rope(proj[:, nq + s0:nq + s0 + LANES], cos, sin, first_half), s0)

    kk = proj[:, nq + nqi:nq + nqi + LANES]
    k_ms = jnp.sum(jnp.where(low_head, kk * kk, 0.0), axis=1, keepdims=True) * (1.0 / ATT_HEAD_DIM)
    kk = jnp.where(low_head, kk * lax.rsqrt(k_ms + EPS) * gk_ref[...], kk)
    kk = _rope(kk, cos, sin, first_half)
    ka_ref[...] = kk.astype(BF16)
    kb_ref[...] = pltpu.roll(kk, ATT_HEAD_DIM, 1).astype(BF16)
    vv = proj[:, nq + nqi + LANES:nq + nqi + 2 * LANES]
    for c in range(tm // tk):
        vt_ref[c] = vv[c * tk:(c + 1) * tk, :].T[:ATT_HEAD_DIM, :].astype(BF16)
    wt_ref[...] = proj[:, nq + nqi + 2 * LANES:].T[:SUBLANES, :]


def _sa_proj(x2d, g, w, gq, gk, rope, *, tm, tk):
    t, d = x2d.shape
    nq = ATT_HEADS * ATT_HEAD_DIM
    nqi = IDX_HEADS * IDX_DIM
    row = lambda wd: pl.BlockSpec((tm, wd), lambda i: (i, 0))
    qblocks = lambda wd: jax.ShapeDtypeStruct((t // CHUNK, LANES, wd), BF16)
    qspec = lambda wd: pl.BlockSpec((tm // CHUNK, LANES, wd), lambda i: (i, 0, 0))
    outs = [qblocks(nq), qblocks(nq), qblocks(nqi), qblocks(nqi),
            jax.ShapeDtypeStruct((t, LANES), BF16), jax.ShapeDtypeStruct((t, LANES), BF16),
            jax.ShapeDtypeStruct((t // tk, ATT_HEAD_DIM, tk), BF16),
            jax.ShapeDtypeStruct((SUBLANES, t), F32)]
    head_of_lane = _expander(LANES, nq, ATT_HEAD_DIM)
    expand, seg = _lhs_stack(head_of_lane), _lhs_stack(head_of_lane.T)
    return pl.pallas_call(
        _sa_proj_body,
        out_shape=outs,
        grid=(t // tm,),
        in_specs=[row(d), _resident(g.shape), _resident(w.shape), _resident(gq.shape), _resident(gk.shape),
                  row(LANES), _resident(seg.shape), _resident(expand.shape)],
        out_specs=[qspec(nq), qspec(nq), qspec(nqi), qspec(nqi), row(LANES), row(LANES),
                   pl.BlockSpec((tm // tk, ATT_HEAD_DIM, tk), lambda i: (i, 0, 0)),
                   pl.BlockSpec((SUBLANES, tm), lambda i: (0, i))],
        compiler_params=_params("parallel"),
        name="sa_proj",
    )(x2d, g, w, gq, gk, rope, seg, expand)


INT_MIN = -2 ** 31
NEG_INF_KEY = INT_MIN + 0x7FFFFF
ATT_KEY_CHUNK = 256


def _sortable_key(x):
    bits = pltpu.bitcast(jnp.where(x == 0.0, 0.0, x), I32)
    return bits ^ ((bits >> 31) & 0x7FFFFFFF)


def _dsa_body(qe_ref, qo_ref, qie_ref, qio_ref, wt_ref, ka_ref, kb_ref, vt_ref, o_ref,
              skey, last_s, m_s, l_s, a_s, acc_s, p_s, *, topk, seq_bits):
    blk = CHUNK
    tk = vt_ref.shape[2]
    j = pl.program_id(1)
    n_chunks = lax.div(j * blk + blk + tk - 1, tk)
    row = _iota((tk, LANES), 0)
    q_pos = j * blk + _iota((tk, LANES), 1)

    qe, qo, qie, qio = qe_ref.at[0], qo_ref.at[0], qie_ref.at[0], qio_ref.at[0]
    n_pairs = ATT_HEADS // 2
    n_ipairs = IDX_HEADS // 2

    idx_scale = (IDX_HEADS ** -0.5) * (IDX_DIM ** -0.5)
    wt = wt_ref[...]

    def chunk(ci):
        return pl.ds(pl.multiple_of(ci * tk, tk), tk)

    cw = 2 * LANES

    def score_chunk(ci):
        sc = jnp.zeros((tk, LANES), F32)
        for par, (src, k_ref) in enumerate(((qie, kb_ref), (qio, ka_ref))):
            kc = k_ref[0, chunk(ci), :]
            for c0 in range(0, n_ipairs * blk, cw):
                logits = _dot(kc, src[:, c0:c0 + cw])
                for u in range(cw // blk):
                    head = 2 * (c0 // blk + u) + par
                    sc = sc + jnp.maximum(logits[:, u * blk:(u + 1) * blk], 0.0) * wt[head:head + 1, :]
        sc = sc * idx_scale
        sc = jnp.where(ci * tk + row <= q_pos, sc, -jnp.inf)
        skey[chunk(ci), :] = _sortable_key(sc)

    def score_chunk_pair(i, _):
        score_chunk(2 * i)
        score_chunk(2 * i + 1)
        return 0

    lax.fori_loop(0, lax.shift_right_logical(n_chunks, 1), score_chunk_pair, 0)

    @pl.when((n_chunks & 1) == 1)
    def _():
        score_chunk(n_chunks - 1)

    acc_rows = 4 * SUBLANES

    def count(pred_fn):
        def body(ci, c):
            hit = jnp.where(pred_fn(skey[chunk(ci), :], ci), 1.0, 0.0)
            return c + jnp.sum(hit.reshape(tk // acc_rows, acc_rows, LANES), axis=0)
        part = lax.fori_loop(0, n_chunks, body, jnp.zeros((acc_rows, LANES), F32))
        return jnp.sum(part, axis=0, keepdims=True)

    kf = float(topk)
    thr = jnp.full((1, LANES), INT_MIN, I32)
    n_ge = jnp.full((1, LANES), 1.0, F32) * (n_chunks * tk).astype(F32)
    for bit in range(31, -1, -1):
        cand = thr + jnp.int32(INT_MIN if bit == 31 else 1 << bit)
        c = count(lambda key, ci, cand=cand: key >= cand)
        thr = jnp.where(c >= kf, cand, thr)
        n_ge = jnp.where(c >= kf, c, n_ge)

    last_s[...] = jnp.full(last_s.shape, 2 ** 31 - 1, I32)
    tied = (n_ge > kf) & (thr > NEG_INF_KEY)

    @pl.when(jnp.sum(jnp.where(tied, 1.0, 0.0)) > 0.0)
    def _():
        need = kf - count(lambda key, ci: key > thr)
        last = jnp.zeros((1, LANES), I32)
        for bit in range(seq_bits - 1, -1, -1):
            cand = last + jnp.int32(1 << bit)
            c = count(lambda key, ci, cand=cand: (key == thr) & (ci * tk + row < cand))
            last = jnp.where(c <= need - 1.0, cand, last)
        last_s[...] = jnp.broadcast_to(last, last_s.shape)

    last = last_s[0:1, :]

    m_s[...] = jnp.full(m_s.shape, -1e30, F32)
    l_s[...] = jnp.zeros(l_s.shape, F32)
    acc_s[...] = jnp.zeros(acc_s.shape, F32)
    half = n_pairs * blk

    def masked_logits(ci):
        key = skey[chunk(ci), :]
        pos = ci * tk + row
        sel = ((key > thr) | ((key == thr) & (pos <= last))) & (pos <= q_pos)
        bias = jnp.where(sel, 0.0, -jnp.inf)
        bias = jnp.concatenate([bias] * (cw // blk), axis=1)
        for src, k_ref, base in ((qe, ka_ref, 0), (qo, kb_ref, half)):
            kc = k_ref[0, chunk(ci), :]
            for c0 in range(0, half, cw):
                yield base + c0, _dot(kc, src[:, c0:c0 + cw]) + bias

    def probabilities(ci):
        m_chunk = [jnp.max(s, axis=0, keepdims=True) for _, s in masked_logits(ci)]
        m_prev = m_s[...]
        m_cur = jnp.maximum(m_prev, jnp.concatenate(m_chunk, axis=1))
        alpha = jnp.exp2(m_prev - m_cur)
        l_chunk = []
        for c0, s in masked_logits(ci):
            p = jnp.exp2(s - m_cur[:, c0:c0 + cw])
            l_chunk.append(jnp.sum(p, axis=0, keepdims=True))
            p_s[:, c0:c0 + cw] = p.astype(BF16)
        l_s[...] = alpha * l_s[...] + jnp.concatenate(l_chunk, axis=1)
        m_s[...] = m_cur
        a_s[...] = alpha

    def accumulate(ci):
        acc_s[...] = a_s[...] * acc_s[...] + _dot(vt_ref[ci], p_s[...])

    def attn_chunk(ci):
        accumulate(ci - 1)
        probabilities(ci)

    def attn_chunk_pair(i, _):
        attn_chunk(2 * i + 1)
        attn_chunk(2 * i + 2)
        return 0

    probabilities(jnp.int32(0))
    n_rest = n_chunks - 1
    lax.fori_loop(0, lax.shift_right_logical(n_rest, 1), attn_chunk_pair, 0)

    @pl.when((n_rest & 1) == 1)
    def _():
        attn_chunk(n_rest)

    accumulate(n_rest)

    out = acc_s[...] / l_s[...]
    for p in range(n_pairs):
        pair = jnp.concatenate([out[:, p * blk:(p + 1) * blk], out[:, half + p * blk:half + (p + 1) * blk]], axis=0)
        o_ref[0, :, p * LANES:(p + 1) * LANES] = pair.T.astype(o_ref.dtype)


def _dsa(qe, qo, qie, qio, wt, ka, kb, vt, *, topk):
    b, l, _ = ka.shape
    nb = l // CHUNK
    tk = vt.shape[2]
    rows = (ATT_HEADS // 2) * CHUNK
    nq = ATT_HEADS * ATT_HEAD_DIM
    qop = lambda a: pl.BlockSpec((1,) + a.shape[1:], lambda i, j: (i * nb + j, 0, 0))
    seq = pl.BlockSpec((1, l, LANES), lambda i, j: (i, 0, 0))
    return pl.pallas_call(
        functools.partial(_dsa_body, topk=topk, seq_bits=int(np.log2(l))),
        out_shape=jax.ShapeDtypeStruct((b, l, nq), BF16),
        grid=(b, nb),
        in_specs=[qop(qe), qop(qo), qop(qie), qop(qio),
                  pl.BlockSpec((SUBLANES, CHUNK), lambda i, j: (0, i * nb + j)),
                  seq, seq, pl.BlockSpec((l // tk, ATT_HEAD_DIM, tk), lambda i, j: (i, 0, 0))],
        out_specs=pl.BlockSpec((1, CHUNK, nq), lambda i, j: (i, j, 0)),
        scratch_shapes=[pltpu.VMEM((l, LANES), I32), pltpu.VMEM((SUBLANES, LANES), I32),
                        pltpu.VMEM((1, 2 * rows), F32), pltpu.VMEM((1, 2 * rows), F32),
                        pltpu.VMEM((1, 2 * rows), F32), pltpu.VMEM((ATT_HEAD_DIM, 2 * rows), F32),
                        pltpu.VMEM((tk, 2 * rows), BF16)],
        compiler_params=_params("parallel", "arbitrary"),
        name="dsa",
    )(qe, qo, qie, qio, wt, ka, kb, vt)


def _split_cols(w, sizes):
    out, c0 = [], 0
    for s in sizes:
        out.append(w[:, c0:c0 + s])
        c0 += s
    return out


def _pad_cols(w, width):
    return jnp.pad(w, ((0, 0), (0, width - w.shape[1])))


def _pad_row(v, width):
    return jnp.pad(v, (0, width - v.shape[0]))[None, :]


def _hybrid_mixer_parts(x2d, b, l, mix_norm, w_in, conv_w, conv_b, dt_bias, a_log, d_skip, ssd_norm,
                        igate_bias, fgate_bias, mlstm_norm, *, tm):
    wz, wxbc, wdt, wq, wk, wv, wo, wi, wf = _split_cols(w_in, HY_SPLITS)
    w = jnp.concatenate([wz, wxbc, wq, wk, wv, wo, _pad_cols(wdt, LANES), _pad_cols(wi, LANES),
                         _pad_cols(wf, LANES)], axis=1).astype(BF16)
    z, xc, qm, km, vm, om, dts, ig, fg = _hy_proj(
        x2d, mix_norm[None, :], w, conv_w, conv_b[None, :], _pad_row(dt_bias, LANES), _pad_row(igate_bias, LANES),
        _pad_row(fgate_bias, LANES), l, tm=tm)
    acts = [a.reshape(b, l, a.shape[1]) for a in (z, xc, dts, qm, km, vm, om, ig, fg)]
    params = [_pad_row(a_log, LANES), jnp.repeat(d_skip, SSD_HEAD_DIM)[None, :], ssd_norm[None, :],
              mlstm_norm[None, :]]
    y, hm = _mixer(acts, params)
    return y.reshape(b * l, SSD_WIDTH), hm.reshape(b * l, ML_WIDTH)


def _rope_table(positions):
    inv = ROPE_THETA ** (-jnp.arange(0, ATT_HEAD_DIM, 2, dtype=F32) / ATT_HEAD_DIM)
    ang = positions.astype(F32)[..., None] * inv
    table = jnp.concatenate([jnp.cos(ang), jnp.sin(ang), jnp.zeros_like(ang), jnp.zeros_like(ang)], axis=-1)
    return table.reshape(-1, LANES)


def _sparse_attention_parts(x2d, b, l, rope, mix_norm, w_in, q_norm, k_norm, topk, *, tm):
    wq, wk, wv, wqi, wki, wwi = _split_cols(w_in, SA_SPLITS)
    w = jnp.concatenate([wq, wqi, wk, wki, _pad_cols(wv, LANES), _pad_cols(wwi, LANES)], axis=1).astype(BF16)
    gq = jnp.tile(q_norm, ATT_HEADS)[None, :]
    gk = _pad_row(k_norm, LANES)
    qe, qo, qie, qio, ka, kb, vt, wt = _sa_proj(x2d, mix_norm[None, :], w, gq, gk, rope, tm=tm, tk=ATT_KEY_CHUNK)
    r3 = lambda a: a.reshape(b, l, a.shape[1])
    o = _dsa(qe, qo, qie, qio, wt, r3(ka), r3(kb), vt, topk=topk)
    return o.reshape(b * l, ATT_HEADS * ATT_HEAD_DIM)


def kernel(x, positions, ffn_norm, ffn_w_gate, ffn_w_up, ffn_w_down, mix_norm, hy_w_in, hy_conv_w, hy_conv_b,
           hy_dt_bias, hy_a_log, hy_d_skip, hy_ssd_norm, hy_igate_bias, hy_fgate_bias, hy_mlstm_norm, hy_w_out,
           sa_w_in, sa_q_norm, sa_k_norm, sa_w_out):
    b, l, d = x.shape
    depth = ffn_norm.shape[0]
    t = b * l
    tm = 512 if t % 512 == 0 else CHUNK
    topk = min(TOPK_MAX, l // 4)
    rope = _rope_table(positions)
    wg, wu, wd = ffn_w_gate, ffn_w_up, ffn_w_down
    x2d = x.reshape(t, d)
    for layer in range(depth):
        x2d = _ffn(x2d, ffn_norm[layer, 0][None, :], wg, wu, wd, layer, 0, tm=tm)
        if layer % 2 == 0:
            e = layer // 2
            y, hm = _hybrid_mixer_parts(x2d, b, l, mix_norm[layer], hy_w_in[e], hy_conv_w[e], hy_conv_b[e],
                                        hy_dt_bias[e], hy_a_log[e], hy_d_skip[e], hy_ssd_norm[e],
                                        hy_igate_bias[e], hy_fgate_bias[e], hy_mlstm_norm[e], tm=tm)
            w_out = hy_w_out[e].astype(BF16)
            acts, ws = [y, hm], [w_out[:SSD_WIDTH], w_out[SSD_WIDTH:]]
        else:
            o = layer // 2
            att = _sparse_attention_parts(x2d, b, l, rope, mix_norm[layer], sa_w_in[o], sa_q_norm[o],
                                          sa_k_norm[o], topk, tm=tm)
            acts, ws = [att], [sa_w_out[o].astype(BF16)]
        x2d = _proj_ffn(x2d, acts, ws, ffn_norm[layer, 1][None, :], wg, wu, wd, layer, 1, tm=tm)
    return x2d.reshape(b, l, d)
```
